```python
import math
import jax, jax.numpy as jnp
from jax import lax
import numpy as np

D_MODEL = 1024
BATCH = 8
SEQ = 8192
DEPTH = 1
DEC_BATCH = 16
DEC_SEQ = 16
PAST_LEN = 1024

CHUNK = 64
N_HEADS_SB = 16
HEAD_DIM = 64
D_SB = N_HEADS_SB * HEAD_DIM
D_RNN = D_MODEL
N_RNN_BLOCKS = 16
RNN_BLOCK = D_RNN // N_RNN_BLOCKS
CONV_WIDTH = 4
RG_C = 8.0
D_FF = -(-8 * D_MODEL // (3 * 256)) * 256
Q_BLOCK = 128
N_BRANCHES = 2
D_IN = 3 * D_SB + 2 * D_RNN + N_BRANCHES * D_MODEL
ALPHA = (2.0 * DEPTH) ** 0.25
BETA = (8.0 * DEPTH) ** -0.25
LN_EPS = 1e-5

kernel_name = "hybrid_stickbreak_rglru_stream_step"


def layer_norm(x, g, b):
    xf = x.astype(jnp.float32)
    mu = jnp.mean(xf, axis=-1, keepdims=True)
    var = jnp.mean(jnp.square(xf - mu), axis=-1, keepdims=True)
    return ((xf - mu) * lax.rsqrt(var + LN_EPS) * g.astype(jnp.float32) + b.astype(jnp.float32)).astype(x.dtype)


def split_input(x, w_in):
    bsz, t, _ = x.shape
    u = jnp.einsum('btd,de->bte', x, w_in)
    offs = [D_SB, 2 * D_SB, 3 * D_SB, 3 * D_SB + D_RNN, 3 * D_SB + 2 * D_RNN]
    q, k, v, xr, xg, gl = jnp.split(u, offs, axis=-1)
    hs = (bsz, t, N_HEADS_SB, HEAD_DIM)
    return q.reshape(hs), k.reshape(hs), v.reshape(hs), xr, xg, gl


def stick_breaking(q, k, v, q_start):
    tq, tk = q.shape[1], k.shape[1]
    logits = jnp.einsum('bqhd,bkhd->bhqk', q.astype(jnp.float32), k.astype(jnp.float32)) * (HEAD_DIM ** -0.5)
    q_pos = q_start + jnp.arange(tq)
    k_pos = jnp.arange(tk)
    mask = k_pos[None, :] < q_pos[:, None]
    log_beta = jax.nn.log_sigmoid(logits)
    log_keep = jnp.where(mask, log_beta - logits, 0.0)
    later = lax.cumsum(log_keep, axis=3, reverse=True) - log_keep
    w = jnp.where(mask, jnp.exp(log_beta + later), 0.0)
    return jnp.einsum('bhqk,bkhd->bqhd', w, v.astype(jnp.float32)).astype(v.dtype)


def rglru_branch(xr, xg, conv_buf, h0, conv_w, conv_b, w_rg_a, b_rg_a, w_rg_x, b_rg_x, lam):
    bsz, t, _ = xr.shape
    xpad = jnp.concatenate([conv_buf.astype(xr.dtype), xr], axis=1)
    new_buf = xpad[:, xpad.shape[1] - (CONV_WIDTH - 1):]
    xpf = xpad.astype(jnp.float32)
    cw = conv_w.astype(jnp.float32)
    xc = conv_b.astype(jnp.float32) + cw[0] * xpf[:, 0:t]
    for j in range(1, CONV_WIDTH):
        xc = xc + cw[j] * xpf[:, j:j + t]
    xb = xc.reshape(bsz, t, N_RNN_BLOCKS, RNN_BLOCK)
    r = jax.nn.sigmoid(jnp.einsum('btnc,ncd->btnd', xb, w_rg_a.astype(jnp.float32)) + b_rg_a.astype(jnp.float32)).reshape(bsz, t, D_RNN)
    i = jax.nn.sigmoid(jnp.einsum('btnc,ncd->btnd', xb, w_rg_x.astype(jnp.float32)) + b_rg_x.astype(jnp.float32)).reshape(bsz, t, D_RNN)
    log_a = -RG_C * r * jax.nn.softplus(-lam.astype(jnp.float32))
    a = jnp.exp(log_a)
    b = jnp.sqrt(-jnp.expm1(2.0 * log_a)) * (i * xc)
    b = b.at[:, 0].add(a[:, 0] * h0.astype(jnp.float32))

    def combine(left, right):
        a_l, b_l = left
        a_r, b_r = right
        return a_l * a_r, a_r * b_l + b_r

    _, h = lax.associative_scan(combine, (a, b), axis=1)
    y = h * jax.nn.gelu(xg.astype(jnp.float32))
    return y.astype(xr.dtype), new_buf, h[:, -1].astype(xr.dtype)


def merge_and_ffn(x, y_sb, y_rg, gl, b_merge, w_sb_proj, w_rg_proj, w_o, ln1_g, ln1_b,
                  w_gate, w_up, w_down, ln2_g, ln2_b):
    bsz, t, _ = x.shape
    gates = jax.nn.sigmoid(gl + b_merge).reshape(bsz, t, N_BRANCHES, D_MODEL)
    ya = jnp.einsum('bte,ed->btd', y_sb.reshape(bsz, t, D_SB), w_sb_proj)
    yb = jnp.einsum('bte,ed->btd', y_rg, w_rg_proj)
    m = gates[:, :, 0] * ya + gates[:, :, 1] * yb
    h1 = layer_norm(ALPHA * x + jnp.einsum('btd,de->bte', m, w_o), ln1_g, ln1_b)
    f = jax.nn.silu(jnp.einsum('btd,df->btf', h1, w_gate)) * jnp.einsum('btd,df->btf', h1, w_up)
    f = jnp.einsum('btf,fd->btd', f, w_down)
    return layer_norm(ALPHA * h1 + f, ln2_g, ln2_b)


def setup_inputs(seed: int = 0) -> dict:
    key = jax.random.key(seed)
    ks = jax.random.split(key, 32)
    f32 = jnp.float32
    nrm = lambda k, shape, s: (jax.random.normal(k, shape, f32) * s)
    a0 = jax.random.uniform(ks[16], (D_RNN,), f32, 0.9, 0.999)
    return {
        "x_prompt": nrm(ks[0], (BATCH, SEQ, D_MODEL), 1.0),
        "x_sample": nrm(ks[1], (DEC_BATCH, DEC_SEQ, D_MODEL), 1.0),
        "cache_k": nrm(ks[2], (DEC_BATCH, PAST_LEN, N_HEADS_SB, HEAD_DIM), 1.0),
        "cache_v": nrm(ks[3], (DEC_BATCH, PAST_LEN, N_HEADS_SB, HEAD_DIM), 1.0),
        "state_conv": nrm(ks[4], (DEC_BATCH, CONV_WIDTH - 1, D_RNN), 1.0),
        "state_h": nrm(ks[5], (DEC_BATCH, D_RNN), 0.5),
        "w_in": nrm(ks[6], (D_MODEL, D_IN), D_MODEL ** -0.5),
        "b_merge": nrm(ks[7], (N_BRANCHES * D_MODEL,), 0.01),
        "w_sb_proj": nrm(ks[8], (D_SB, D_MODEL), D_SB ** -0.5),
        "w_rg_proj": nrm(ks[9], (D_RNN, D_MODEL), D_RNN ** -0.5),
        "conv_w": nrm(ks[10], (CONV_WIDTH, D_RNN), CONV_WIDTH ** -0.5),
        "conv_b": nrm(ks[11], (D_RNN,), 0.01),
        "w_rg_a": nrm(ks[12], (N_RNN_BLOCKS, RNN_BLOCK, RNN_BLOCK), RNN_BLOCK ** -0.5),
        "b_rg_a": nrm(ks[13], (N_RNN_BLOCKS, RNN_BLOCK), 0.01),
        "w_rg_x": nrm(ks[14], (N_RNN_BLOCKS, RNN_BLOCK, RNN_BLOCK), RNN_BLOCK ** -0.5),
        "b_rg_x": nrm(ks[15], (N_RNN_BLOCKS, RNN_BLOCK), 0.01),
        "lam": jnp.log(a0) - jnp.log1p(-a0),
        "w_o": nrm(ks[17], (D_MODEL, D_MODEL), BETA * D_MODEL ** -0.5),
        "ln1_g": 1.0 + nrm(ks[18], (D_MODEL,), 0.02),
        "ln1_b": nrm(ks[19], (D_MODEL,), 0.02),
        "w_gate": nrm(ks[20], (D_MODEL, D_FF), D_MODEL ** -0.5),
        "w_up": nrm(ks[21], (D_MODEL, D_FF), D_MODEL ** -0.5),
        "w_down": nrm(ks[22], (D_FF, D_MODEL), BETA * D_FF ** -0.5),
        "ln2_g": 1.0 + nrm(ks[23], (D_MODEL,), 0.02),
        "ln2_b": nrm(ks[24], (D_MODEL,), 0.02),
    }


def reference(x_prompt, x_sample, cache_k, cache_v, state_conv, state_h,
              w_in, b_merge, w_sb_proj, w_rg_proj, conv_w, conv_b, w_rg_a, b_rg_a,
              w_rg_x, b_rg_x, lam, w_o, ln1_g, ln1_b, w_gate, w_up, w_down, ln2_g, ln2_b):
    bp, tp, _ = x_prompt.shape
    q, k_p, v_p, xr, xg, gl = split_input(x_prompt, w_in)
    blocks = []
    for s in range(0, tp, Q_BLOCK):
        e = min(s + Q_BLOCK, tp)
        blocks.append(stick_breaking(q[:, s:e], k_p[:, :e], v_p[:, :e], s))
    y_sb = jnp.concatenate(blocks, axis=1)
    zero_buf = jnp.zeros((bp, CONV_WIDTH - 1, D_RNN), x_prompt.dtype)
    zero_h = jnp.zeros((bp, D_RNN), x_prompt.dtype)
    y_rg, conv_p, h_p = rglru_branch(xr, xg, zero_buf, zero_h, conv_w, conv_b,
                                     w_rg_a, b_rg_a, w_rg_x, b_rg_x, lam)
    y_prompt = merge_and_ffn(x_prompt, y_sb, y_rg, gl, b_merge, w_sb_proj, w_rg_proj, w_o,
                             ln1_g, ln1_b, w_gate, w_up, w_down, ln2_g, ln2_b)

    past = cache_k.shape[1]
    q_s, k_s, v_s, xr_s, xg_s, gl_s = split_input(x_sample, w_in)
    k_all = jnp.concatenate([cache_k.astype(k_s.dtype), k_s], axis=1)
    v_all = jnp.concatenate([cache_v.astype(v_s.dtype), v_s], axis=1)
    y_sb_s = stick_breaking(q_s, k_all, v_all, past)
    y_rg_s, conv_s, h_s = rglru_branch(xr_s, xg_s, state_conv, state_h, conv_w, conv_b,
                                       w_rg_a, b_rg_a, w_rg_x, b_rg_x, lam)
    y_sample = merge_and_ffn(x_sample, y_sb_s, y_rg_s, gl_s, b_merge, w_sb_proj, w_rg_proj, w_o,
                             ln1_g, ln1_b, w_gate, w_up, w_down, ln2_g, ln2_b)

    return (y_prompt, y_sample, k_p, v_p, conv_p, h_p, k_s, v_s, conv_s, h_s)
```

```python
import functools
import math

import jax
import jax.numpy as jnp
from jax import lax
from jax.experimental import pallas as pl
from jax.experimental.pallas import tpu as pltpu

F32 = jnp.float32
BF16 = jnp.bfloat16

N_HEADS = 16
HEAD_DIM = 64
LANES = 128
HEADS_PER_SLAB = LANES // HEAD_DIM
N_RNN_BLOCKS = 16
CONV_WIDTH = 4
RG_C = 8.0
LN_EPS = 1e-5
N_BRANCHES = 2
GATE_CHUNK = 256
F32_EXP_UNDERFLOW = 104.0
VMEM_LIMIT = 56 * 1024 * 1024


def _params(sem):
    return pltpu.CompilerParams(dimension_semantics=sem, vmem_limit_bytes=VMEM_LIMIT)


def _resident(shape):
    nd = len(shape)
    return pl.BlockSpec(shape, lambda *_: (0,) * nd, pipeline_mode=pl.Buffered(1))


def _in_proj_kernel(x_ref, w_ref, q_ref, k_ref, v_ref, kb_ref, vb_ref, xr_ref, xg_ref, gl_ref, *, d):
    xb = x_ref[...].astype(BF16)

    def mm(c0, c1):
        return jnp.dot(xb, w_ref[:, c0:c1], preferred_element_type=F32)

    q_ref[...] = (mm(0, d) * (HEAD_DIM ** -0.5)).astype(BF16)
    k = mm(d, 2 * d)
    k_ref[...] = k
    kb_ref[...] = k.astype(BF16)
    v = mm(2 * d, 3 * d)
    v_ref[...] = v
    vb_ref[...] = v.astype(BF16)
    xr_ref[...] = mm(3 * d, 4 * d)
    xg_ref[...] = mm(4 * d, 5 * d)
    gl_ref[...] = mm(5 * d, 7 * d)


def _in_proj(x2d, w_in_bf, tm):
    n, d = x2d.shape
    assert n % tm == 0
    row = lambda w: pl.BlockSpec((tm, w), lambda i: (i, 0))
    outs = [(d, BF16), (d, F32), (d, F32), (d, BF16), (d, BF16), (d, F32), (d, F32), (2 * d, F32)]
    return pl.pallas_call(
        functools.partial(_in_proj_kernel, d=d),
        grid=(n // tm,),
        in_specs=[row(d), _resident(w_in_bf.shape)],
        out_specs=[row(w) for w, _ in outs],
        out_shape=[jax.ShapeDtypeStruct((n, w), dt) for w, dt in outs],
        compiler_params=_params(("parallel",)),
        name="in_proj",
    )(x2d, w_in_bf)


def _attn_kernel(q_ref, k_ref, v_ref, tri_ref, o_ref, acc_ref, carry_ref, *, q_start, tqb, tq, tk):
    step = pl.program_id(1)
    qs_blk = q_start + step * tqb
    acc_ref[...] = jnp.zeros_like(acc_ref)
    carry_ref[...] = jnp.zeros_like(carry_ref)

    row_iota = lax.broadcasted_iota(jnp.int32, (tq, 1), 0)
    col_iota = lax.broadcasted_iota(jnp.int32, (tq, tk), 1)
    lane_iota = lax.broadcasted_iota(jnp.int32, (tq, LANES), 1)
    first_head = lane_iota < HEAD_DIM
    tri = tri_ref[...]

    def window_body(state):
        w, _ = state

        def subtile_body(j, cmin):
            row0 = pl.multiple_of(j * tq, tq)
            qs = qs_blk + j * tq
            hi = jnp.maximum(qs + tq - w * tk, 0)
            s = pl.multiple_of(jnp.maximum(qs + tq - (w + 1) * tk, 0), 16)
            mask = col_iota < (jnp.minimum(qs + row_iota, hi) - s)
            rows = pl.ds(row0, tq)
            for slab in range(N_HEADS // HEADS_PER_SLAB):
                lanes = slice(slab * LANES, (slab + 1) * LANES)
                q2 = q_ref[rows, lanes]
                k2 = k_ref[pl.ds(s, tk), lanes]
                v2 = v_ref[pl.ds(s, tk), lanes]
                outs = []
                for half in range(HEADS_PER_SLAB):
                    h = slab * HEADS_PER_SLAB + half
                    sel = first_head if half == 0 else jnp.logical_not(first_head)
                    qh = jnp.where(sel, q2, jnp.zeros_like(q2))
                    z = lax.dot_general(qh, k2, (((1,), (1,)), ((), ())), preferred_element_type=F32)
                    sp = jnp.maximum(z, 0.0) + jnp.log1p(jnp.exp(-jnp.abs(z)))
                    spm = jnp.where(mask, sp, 0.0)
                    sp_hi = spm.astype(BF16)
                    sp_lo = (spm - sp_hi.astype(F32)).astype(BF16)
                    later = jnp.dot(jnp.concatenate([sp_hi, sp_lo], axis=1), tri, preferred_element_type=F32)
                    c_old = carry_ref[rows, h:h + 1]
                    wgt = jnp.where(mask, jnp.exp(z - sp - later - c_old), 0.0).astype(BF16)
                    outs.append(jnp.dot(wgt, v2, preferred_element_type=F32))
                    c_new = c_old + jnp.sum(spm, axis=1, keepdims=True)
                    carry_ref[rows, h:h + 1] = c_new
                    cmin = jnp.minimum(cmin, c_new)
                acc_ref[rows, lanes] += jnp.where(first_head, outs[0], outs[1])
            return cmin

        cmin = lax.fori_loop(0, tqb // tq, subtile_body, jnp.full((tq, 1), jnp.inf, F32))
        oldest_unvisited = qs_blk + tqb - (w + 1) * tk
        more = jnp.logical_and(jnp.min(cmin) < F32_EXP_UNDERFLOW, oldest_unvisited > 0)
        return w + 1, more

    lax.while_loop(lambda st: st[1], window_body, (jnp.int32(0), jnp.bool_(True)))
    o_ref[...] = acc_ref[...].astype(o_ref.dtype)


def _tri(tk):
    j = lax.broadcasted_iota(jnp.int32, (tk, tk), 0)
    s = lax.broadcasted_iota(jnp.int32, (tk, tk), 1)
    m = (j > s).astype(BF16)
    return jnp.concatenate([m, m], axis=0)


def _attention(q, k, v, q_start, tqb, tq, tk):
    b, t_q, d = q.shape
    t_k = k.shape[1]
    assert t_k == q_start + t_q and t_k >= tk
    assert t_q % tqb == 0 and tqb % tq == 0 and tq % 16 == 0 and tk % 16 == 0 and q_start % 16 == 0
    kv_spec = pl.BlockSpec((None, t_k, d), lambda bi, i: (bi, 0, 0), pipeline_mode=pl.Buffered(1))
    return pl.pallas_call(
        functools.partial(_attn_kernel, q_start=q_start, tqb=tqb, tq=tq, tk=tk),
        grid=(b, t_q // tqb),
        in_specs=[pl.BlockSpec((None, tqb, d), lambda bi, i: (bi, i, 0)), kv_spec, kv_spec,
                  _resident((2 * tk, tk))],
        out_specs=pl.BlockSpec((None, tqb, d), lambda bi, i: (bi, i, 0)),
        out_shape=jax.ShapeDtypeStruct((b, t_q, d), BF16),
        scratch_shapes=[pltpu.VMEM((tqb, d), F32), pltpu.VMEM((tqb, N_HEADS), F32)],
        compiler_params=_params(("parallel", "arbitrary")),
        name="stickbreak_attn",
    )(q, k, v, _tri(tk))


def _rglru_kernel(xr_ref, xg_ref, conv0_ref, h0_ref, cw_ref, cb_ref, wg_ref, bg_ref, lam_ref,
                  y_ref, hlast_ref, xpad_ref, a_ref, b_ref, h_ref, *, tt, d):
    i = pl.program_id(1)
    pad = 8
    hist = CONV_WIDTH - 1

    @pl.when(i == 0)
    def _():
        xpad_ref[pad - hist:pad, :] = conv0_ref[...]
        h_ref[...] = h0_ref[...]

    @pl.when(i > 0)
    def _():
        xpad_ref[pad - hist:pad, :] = xpad_ref[pad + tt - hist:pad + tt, :]

    xpad_ref[pad:pad + tt, :] = xr_ref[...]
    xc = cb_ref[...] + cw_ref[0:1, :] * xpad_ref[pad - hist:pad - hist + tt, :]
    for j in range(1, CONV_WIDTH):
        xc = xc + cw_ref[j:j + 1, :] * xpad_ref[pad - hist + j:pad - hist + j + tt, :]

    xcb = xc.astype(BF16)
    nchunk = d // GATE_CHUNK
    gates = [jnp.dot(xcb[:, c * GATE_CHUNK:(c + 1) * GATE_CHUNK], wg_ref[c], preferred_element_type=F32)
             for c in range(nchunk)]
    r = jax.nn.sigmoid(jnp.concatenate([g[:, :GATE_CHUNK] for g in gates], axis=1) + bg_ref[0:1, :])
    ig = jax.nn.sigmoid(jnp.concatenate([g[:, GATE_CHUNK:] for g in gates], axis=1) + bg_ref[1:2, :])
    log_a = (-RG_C) * r * jax.nn.softplus(-lam_ref[...])
    a = jnp.exp(log_a)
    a_ref[...] = a
    b_ref[...] = jnp.sqrt(jnp.tanh(-log_a) * (a * a + 1.0)) * (ig * xc)

    sub = lax.broadcasted_iota(jnp.int32, (8, d), 0)

    def group(g, hprev):
        rows = pl.ds(pl.multiple_of(g * 8, 8), 8)
        a = a_ref[rows, :]
        b = b_ref[rows, :]
        for sh in (1, 2, 4):
            keep = sub >= sh
            a_sh = jnp.where(keep, pltpu.roll(a, sh, 0), 1.0)
            b_sh = jnp.where(keep, pltpu.roll(b, sh, 0), 0.0)
            b = a * b_sh + b
            a = a * a_sh
        h = a * hprev + b
        b_ref[rows, :] = h
        return h[7:8, :]

    hlast = lax.fori_loop(0, tt // 8, group, h_ref[...])
    h_ref[...] = hlast
    hlast_ref[...] = hlast
    y_ref[...] = (b_ref[...] * jax.nn.gelu(xg_ref[...])).astype(y_ref.dtype)


def _pack_gate_weights(w_a, w_x):
    nb, blk, _ = w_a.shape
    per = GATE_CHUNK // blk

    def bd(w):
        w = w.reshape(nb // per, per, blk, blk)
        eye = jnp.eye(per, dtype=w.dtype)
        return jnp.einsum("cpij,pq->cpiqj", w, eye).reshape(nb // per, GATE_CHUNK, GATE_CHUNK)

    return jnp.concatenate([bd(w_a), bd(w_x)], axis=2).astype(BF16)


def _rglru(xr, xg, conv0, h0, conv_w, conv_b, w_rg_a, b_rg_a, w_rg_x, b_rg_x, lam, tt):
    b, t, d = xr.shape
    assert t % tt == 0 and tt % 8 == 0 and t >= CONV_WIDTH - 1
    wg = _pack_gate_weights(w_rg_a, w_rg_x)
    bg = jnp.stack([b_rg_a.reshape(d), b_rg_x.reshape(d)])
    seq = pl.BlockSpec((None, tt, d), lambda bi, i: (bi, i, 0))
    y, hlast = pl.pallas_call(
        functools.partial(_rglru_kernel, tt=tt, d=d),
        grid=(b, t // tt),
        in_specs=[seq, seq,
                  pl.BlockSpec((None, CONV_WIDTH - 1, d), lambda bi, i: (bi, 0, 0)),
                  pl.BlockSpec((None, 1, d), lambda bi, i: (bi, 0, 0)),
                  _resident((CONV_WIDTH, d)), _resident((1, d)), _resident(wg.shape),
                  _resident((2, d)), _resident((1, d))],
        out_specs=[seq, pl.BlockSpec((None, 1, d), lambda bi, i: (bi, 0, 0))],
        out_shape=[jax.ShapeDtypeStruct((b, t, d), BF16), jax.ShapeDtypeStruct((b, 1, d), F32)],
        scratch_shapes=[pltpu.VMEM((tt + 8, d), F32), pltpu.VMEM((tt, d), F32),
                        pltpu.VMEM((tt, d), F32), pltpu.VMEM((1, d), F32)],
        compiler_params=_params(("parallel", "arbitrary")),
        name="rglru",
    )(xr, xg, conv0, h0.reshape(b, 1, d), conv_w, conv_b.reshape(1, d), wg, bg, lam.reshape(1, d))
    return y, hlast.reshape(b, d)


def _layer_norm(x, g, b):
    mu = jnp.mean(x, axis=-1, keepdims=True)
    xc = x - mu
    var = jnp.mean(xc * xc, axis=-1, keepdims=True)
    return xc * lax.rsqrt(var + LN_EPS) * g + b


def _merge_ffn_kernel(x_ref, gl_ref, ysb_ref, yrg_ref, bm_ref, wsb_ref, wrg_ref, wo_ref, ln1_ref,
                      wgu_ref, wd_ref, ln2_ref, y_ref, f_ref, *, d, dff, alpha, fc):
    ya = jnp.dot(ysb_ref[...], wsb_ref[...], preferred_element_type=F32)
    yb = jnp.dot(yrg_ref[...], wrg_ref[...], preferred_element_type=F32)
    g = jax.nn.sigmoid(gl_ref[...] + bm_ref[...])
    m = (g[:, :d] * ya + g[:, d:] * yb).astype(BF16)
    h1 = _layer_norm(alpha * x_ref[...] + jnp.dot(m, wo_ref[...], preferred_element_type=F32),
                     ln1_ref[0:1, :], ln1_ref[1:2, :])
    h1b = h1.astype(BF16)
    for c in range(dff // fc):
        gate = jnp.dot(h1b, wgu_ref[0, :, c * fc:(c + 1) * fc], preferred_element_type=F32)
        up = jnp.dot(h1b, wgu_ref[1, :, c * fc:(c + 1) * fc], preferred_element_type=F32)
        f_ref[:, c * fc:(c + 1) * fc] = (jax.nn.silu(gate) * up).astype(BF16)
    f = jnp.dot(f_ref[...], wd_ref[...], preferred_element_type=F32)
    y_ref[...] = _layer_norm(alpha * h1 + f, ln2_ref[0:1, :], ln2_ref[1:2, :])


def _merge_ffn(x2d, gl, y_sb, y_rg, b_merge, w_sb, w_rg, w_o, ln1, w_gu, w_down, ln2, alpha, tm):
    n, d = x2d.shape
    dff = w_down.shape[0]
    fc = 256
    assert n % tm == 0 and dff % fc == 0
    row = lambda w: pl.BlockSpec((tm, w), lambda i: (i, 0))
    return pl.pallas_call(
        functools.partial(_merge_ffn_kernel, d=d, dff=dff, alpha=alpha, fc=fc),
        grid=(n // tm,),
        in_specs=[row(d), row(N_BRANCHES * d), row(d), row(d),
                  _resident((1, N_BRANCHES * d)), _resident(w_sb.shape), _resident(w_rg.shape),
                  _resident(w_o.shape), _resident((2, d)), _resident(w_gu.shape),
                  _resident(w_down.shape), _resident((2, d))],
        out_specs=row(d),
        out_shape=jax.ShapeDtypeStruct((n, d), F32),
        scratch_shapes=[pltpu.VMEM((tm, dff), BF16)],
        compiler_params=_params(("parallel",)),
        name="merge_ffn",
    )(x2d, gl, y_sb, y_rg, b_merge.reshape(1, -1), w_sb, w_rg, w_o, ln1, w_gu, w_down, ln2)


def _layer(x, past_k, past_v, conv0, h0, wts, *, tm_proj, tqb, tq, tk, tt, tm_ffn):
    b, t, d = x.shape
    x2d = x.reshape(b * t, d)
    q, k, v, kb, vb, xr, xg, gl = _in_proj(x2d, wts["w_in"], tm_proj)
    seq = lambda a: a.reshape(b, t, d)
    kb, vb = seq(kb), seq(vb)
    q_start = 0
    if past_k is not None:
        q_start = past_k.shape[1]
        kb = jnp.concatenate([past_k.reshape(b, q_start, d).astype(BF16), kb], axis=1)
        vb = jnp.concatenate([past_v.reshape(b, q_start, d).astype(BF16), vb], axis=1)
    y_sb = _attention(seq(q), kb, vb, q_start, tqb, tq, min(tk, kb.shape[1]))
    xr3 = seq(xr)
    y_rg, h_last = _rglru(xr3, seq(xg), conv0, h0, wts["conv_w"], wts["conv_b"], wts["w_rg_a"],
                          wts["b_rg_a"], wts["w_rg_x"], wts["b_rg_x"], wts["lam"], tt)
    y = _merge_ffn(x2d, gl, y_sb.reshape(b * t, d), y_rg.reshape(b * t, d), wts["b_merge"],
                   wts["w_sb_proj"], wts["w_rg_proj"], wts["w_o"], wts["ln1"], wts["w_gu"],
                   wts["w_down"], wts["ln2"], wts["alpha"], tm_ffn)
    heads = lambda a: a.reshape(b, t, N_HEADS, HEAD_DIM)
    conv_new = xr3[:, t - (CONV_WIDTH - 1):, :]
    return seq(y), heads(k), heads(v), conv_new, h_last


def kernel(x_prompt, x_sample, cache_k, cache_v, state_conv, state_h, w_in, b_merge, w_sb_proj, w_rg_proj, conv_w, conv_b, w_rg_a, b_rg_a, w_rg_x, b_rg_x, lam, w_o, ln1_g, ln1_b, w_gate, w_up, w_down, ln2_g, ln2_b):
    depth = 1
    wts = dict(
        w_in=w_in.astype(BF16), b_merge=b_merge, w_sb_proj=w_sb_proj.astype(BF16),
        w_rg_proj=w_rg_proj.astype(BF16), conv_w=conv_w, conv_b=conv_b, w_rg_a=w_rg_a, b_rg_a=b_rg_a,
        w_rg_x=w_rg_x, b_rg_x=b_rg_x, lam=lam, w_o=w_o.astype(BF16), ln1=jnp.stack([ln1_g, ln1_b]),
        w_gu=jnp.stack([w_gate, w_up]).astype(BF16), w_down=w_down.astype(BF16),
        ln2=jnp.stack([ln2_g, ln2_b]), alpha=(2.0 * depth) ** 0.25)

    bp, tp, d = x_prompt.shape
    zero_conv = jnp.zeros((bp, CONV_WIDTH - 1, d), x_prompt.dtype)
    zero_h = jnp.zeros((bp, d), x_prompt.dtype)
    y_p, k_p, v_p, conv_p, h_p = _layer(
        x_prompt, None, None, zero_conv, zero_h, wts,
        tm_proj=min(256, bp * tp), tqb=min(256, tp), tq=min(64, tp), tk=256, tt=min(256, tp),
        tm_ffn=min(512, bp * tp))

    bs, ts, _ = x_sample.shape
    y_s, k_s, v_s, conv_s, h_s = _layer(
        x_sample, cache_k, cache_v, state_conv, state_h, wts,
        tm_proj=bs * ts, tqb=ts, tq=ts, tk=256, tt=ts, tm_ffn=bs * ts)

    return (y_p, y_s, k_p, v_p, conv_p, h_p, k_s, v_s, conv_s, h_s)
```

```python
import functools

import jax
import jax.numpy as jnp
from jax import lax
from jax.experimental import pallas as pl
from jax.experimental.pallas import tpu as pltpu

F32 = jnp.float32
BF16 = jnp.bfloat16

N_HEADS = 16
HEAD_DIM = 64
LANES = 128
HEADS_PER_SLAB = LANES // HEAD_DIM
SLABS = N_HEADS // HEADS_PER_SLAB
SLABS_PER_GROUP = 4
KV_CHUNK = 16
CONV_WIDTH = 4
RG_C = 8.0
LN_EPS = 1e-5
N_BRANCHES = 2
GATE_CHUNK = 256
F32_EXP_UNDERFLOW = 104.0
VMEM_LIMIT = 56 * 1024 * 1024


def _params(sem):
    return pltpu.CompilerParams(dimension_semantics=sem, vmem_limit_bytes=VMEM_LIMIT)


def _resident(shape):
    nd = len(shape)
    return pl.BlockSpec(shape, lambda *_: (0,) * nd, pipeline_mode=pl.Buffered(1))


def _in_proj_kernel(x_ref, w_ref, q_ref, k_ref, v_ref, kb_ref, vb_ref, xr_ref, xg_ref, gl_ref, *,
                    d, tm, time_minor):
    xb = x_ref[...].astype(BF16)

    def mm(c0, c1):
        return jnp.dot(xb, w_ref[:, c0:c1], preferred_element_type=F32)

    q_ref[...] = (mm(0, d) * (HEAD_DIM ** -0.5)).astype(BF16)
    for c, (o_ref, ob_ref) in enumerate(((k_ref, kb_ref), (v_ref, vb_ref)), start=1):
        u = mm(c * d, (c + 1) * d)
        ob_ref[...] = u.astype(BF16)
        if time_minor:
            o_ref[...] = u.T
        else:
            for h in range(N_HEADS):
                o_ref[pl.ds(h, tm, stride=N_HEADS), :] = u[:, h * HEAD_DIM:(h + 1) * HEAD_DIM]
    xr_ref[...] = mm(3 * d, 4 * d)
    xg_ref[...] = mm(4 * d, 5 * d)
    gl_ref[...] = mm(5 * d, 7 * d)


def _in_proj(x, w_in_bf, tm, time_minor):
    b, t, d = x.shape
    assert t % tm == 0
    row = lambda w: pl.BlockSpec((None, tm, w), lambda bi, i: (bi, i, 0))
    row_shape = lambda w, dt: jax.ShapeDtypeStruct((b, t, w), dt)
    if time_minor:
        kv_spec = pl.BlockSpec((None, d, tm), lambda bi, i: (bi, 0, i))
        kv_shape = jax.ShapeDtypeStruct((b, d, t), F32)
    else:
        kv_spec = pl.BlockSpec((None, tm * N_HEADS, HEAD_DIM), lambda bi, i: (bi, i, 0))
        kv_shape = jax.ShapeDtypeStruct((b, t * N_HEADS, HEAD_DIM), F32)
    q, k, v, kb, vb, xr, xg, gl = pl.pallas_call(
        functools.partial(_in_proj_kernel, d=d, tm=tm, time_minor=time_minor),
        grid=(b, t // tm),
        in_specs=[row(d), _resident(w_in_bf.shape)],
        out_specs=[row(d), kv_spec, kv_spec, row(d), row(d), row(d), row(d), row(2 * d)],
        out_shape=[row_shape(d, BF16), kv_shape, kv_shape, row_shape(d, BF16), row_shape(d, BF16),
                   row_shape(d, F32), row_shape(d, F32), row_shape(2 * d, F32)],
        compiler_params=_params(("parallel", "parallel")),
        name="in_proj",
    )(x, w_in_bf)
    if time_minor:
        k, v = (a.reshape(b, N_HEADS, HEAD_DIM, t).transpose(0, 3, 1, 2) for a in (k, v))
    else:
        k, v = (a.reshape(b, t, N_HEADS, HEAD_DIM) for a in (k, v))
    return q, k, v, kb, vb, xr, xg, gl


def _attn_subtile(q_ref, k_ref, v_ref, tri, acc_ref, carry_ref, z_ref, hl_ref, *, j, w, qs_blk, tq, tk, first):
    row0 = pl.multiple_of(j * tq, tq)
    rows = pl.ds(row0, tq)
    qs = qs_blk + j * tq
    hi_key = jnp.maximum(qs + tq - w * tk, 0)
    s = jnp.maximum(qs + tq - (w + 1) * tk, 0)
    win = pl.ds(s // KV_CHUNK, tk // KV_CHUNK)
    row_iota = lax.broadcasted_iota(jnp.int32, (tq, 1), 0)
    col_iota = lax.broadcasted_iota(jnp.int32, (tq, tk), 1)
    mask = col_iota < (jnp.minimum(qs + row_iota, hi_key) - s)
    mask2 = jnp.concatenate([mask, mask], axis=0)
    first_head = lax.broadcasted_iota(jnp.int32, (tq, LANES), 1) < HEAD_DIM
    t2 = HEADS_PER_SLAB * tq
    grp_rows = SLABS_PER_GROUP * t2

    later = []
    for grp in range(SLABS // SLABS_PER_GROUP):
        for sl in range(SLABS_PER_GROUP):
            slab = grp * SLABS_PER_GROUP + sl
            lanes = slice(slab * LANES, (slab + 1) * LANES)
            q2 = q_ref[rows, lanes]
            zero = jnp.zeros_like(q2)
            qq = jnp.concatenate([jnp.where(first_head, q2, zero), jnp.where(first_head, zero, q2)], axis=0)
            k2 = k_ref[win, :, lanes].reshape(tk, LANES)
            z = lax.dot_general(qq, k2, (((1,), (1,)), ((), ())), preferred_element_type=F32)
            sp = jnp.where(mask2, jnp.maximum(z, 0.0) + jnp.log(1.0 + jnp.exp(-jnp.abs(z))), 0.0)
            sp_hi = pltpu.bitcast(pltpu.bitcast(sp, jnp.uint32) & jnp.uint32(0xFFFF0000), F32)
            srows = slice(slab * t2, (slab + 1) * t2)
            z_ref[srows, :] = z
            hl_ref[srows, 0:tk] = sp_hi
            hl_ref[srows, tk:2 * tk] = sp - sp_hi
        later.append(lax.dot_general(hl_ref[grp * grp_rows:(grp + 1) * grp_rows, :], tri,
                                     (((1,), (0,)), ((), ())), preferred_element_type=F32))

    cmin = jnp.full((tq, 1), jnp.inf, F32)
    for slab in range(SLABS):
        lanes = slice(slab * LANES, (slab + 1) * LANES)
        sl = slab % SLABS_PER_GROUP
        lat = later[slab // SLABS_PER_GROUP][sl * t2:(sl + 1) * t2, :]
        x = z_ref[slab * t2:(slab + 1) * t2, :] - lat
        tot = lat[:, 0:1]
        ha, hb = slab * HEADS_PER_SLAB, slab * HEADS_PER_SLAB + 1
        if not first:
            c_old = jnp.concatenate([carry_ref[rows, ha:ha + 1], carry_ref[rows, hb:hb + 1]], axis=0)
            x = x - c_old
            tot = tot + c_old
        wgt = jnp.where(mask2, jnp.exp(x), 0.0).astype(BF16)
        o2 = jnp.dot(wgt, v_ref[win, :, lanes].reshape(tk, LANES), preferred_element_type=F32)
        o = jnp.where(first_head, o2[:tq], o2[tq:])
        if first:
            acc_ref[rows, lanes] = o
        else:
            acc_ref[rows, lanes] += o
        carry_ref[rows, ha:ha + 1] = tot[:tq]
        carry_ref[rows, hb:hb + 1] = tot[tq:]
        cmin = jnp.minimum(cmin, jnp.minimum(tot[:tq], tot[tq:]))
    return cmin


def _attn_kernel(q_ref, k_ref, v_ref, tri_ref, o_ref, acc_ref, carry_ref, z_ref, hl_ref, *, q_start, tqb, tq, tk):
    qs_blk = q_start + pl.program_id(1) * tqb
    tri = tri_ref[...]
    sub = functools.partial(_attn_subtile, q_ref, k_ref, v_ref, tri, acc_ref, carry_ref, z_ref, hl_ref,
                            qs_blk=qs_blk, tq=tq, tk=tk)
    inf = jnp.full((tq, 1), jnp.inf, F32)

    def more_keys(w, cmin):
        return jnp.logical_and(jnp.min(cmin) < F32_EXP_UNDERFLOW, qs_blk + tqb - (w + 1) * tk > 0)

    cmin = lax.fori_loop(0, tqb // tq, lambda j, c: jnp.minimum(c, sub(j=j, w=0, first=True)), inf)

    def older_window(state):
        w, _ = state
        cmin = lax.fori_loop(0, tqb // tq, lambda j, c: jnp.minimum(c, sub(j=j, w=w, first=False)), inf)
        return w + 1, more_keys(w, cmin)

    lax.while_loop(lambda st: st[1], older_window, (jnp.int32(1), more_keys(0, cmin)))
    o_ref[...] = acc_ref[...].astype(o_ref.dtype)


def _tri(tk):
    j = lax.broadcasted_iota(jnp.int32, (tk, tk), 0)
    s = lax.broadcasted_iota(jnp.int32, (tk, tk), 1)
    m = (j >= s).astype(BF16)
    return jnp.concatenate([m, m], axis=0)


def _attention(q, k, v, q_start, tqb, tq, tk):
    b, t_q, d = q.shape
    t_k = k.shape[1]
    assert d == N_HEADS * HEAD_DIM and t_k == q_start + t_q and t_k >= tk
    assert t_q % tqb == 0 and tqb % tq == 0 and tq % 16 == 0 and tk % 16 == 0 and q_start % 16 == 0
    assert t_k % KV_CHUNK == 0 and tk % KV_CHUNK == 0
    k, v = (a.reshape(b, t_k // KV_CHUNK, KV_CHUNK, d) for a in (k, v))
    kv_spec = pl.BlockSpec((None, t_k // KV_CHUNK, KV_CHUNK, d), lambda bi, i: (bi, 0, 0, 0),
                           pipeline_mode=pl.Buffered(1))
    return pl.pallas_call(
        functools.partial(_attn_kernel, q_start=q_start, tqb=tqb, tq=tq, tk=tk),
        grid=(b, t_q // tqb),
        in_specs=[pl.BlockSpec((None, tqb, d), lambda bi, i: (bi, i, 0)), kv_spec, kv_spec,
                  _resident((2 * tk, tk))],
        out_specs=pl.BlockSpec((None, tqb, d), lambda bi, i: (bi, i, 0)),
        out_shape=jax.ShapeDtypeStruct((b, t_q, d), BF16),
        scratch_shapes=[pltpu.VMEM((tqb, d), F32), pltpu.VMEM((tqb, N_HEADS), F32),
                        pltpu.VMEM((N_HEADS * tq, tk), F32), pltpu.VMEM((N_HEADS * tq, 2 * tk), F32)],
        compiler_params=_params(("parallel", "arbitrary")),
        name="stickbreak_attn",
    )(q, k, v, _tri(tk))


def _rglru_kernel(xr_ref, xg_ref, conv0_ref, h0_ref, cw_ref, cb_ref, wg_ref, bg_ref, lam_ref,
                  y_ref, hlast_ref, hist_ref, a_ref, b_ref, h_ref, *, tt, d):
    i = pl.program_id(1)
    hist = CONV_WIDTH - 1

    @pl.when(i == 0)
    def _():
        hist_ref[0:hist, :] = conv0_ref[...]
        h_ref[...] = h0_ref[...]

    x = xr_ref[...]
    first_row = lax.broadcasted_iota(jnp.int32, (tt, d), 0) == 0
    u = None
    for j in range(hist):
        u_prev = cw_ref[j:j + 1, :] * hist_ref[hist - 1:hist, :]
        for m in range(j):
            u_prev = u_prev + cw_ref[m:m + 1, :] * hist_ref[hist - 1 - (j - m):hist - (j - m), :]
        u = cw_ref[j:j + 1, :] * x if u is None else cw_ref[j:j + 1, :] * x + u
        u = jnp.where(first_row, u_prev, pltpu.roll(u, 1, 0))
    xc = cb_ref[...] + (cw_ref[hist:hist + 1, :] * x + u)
    hist_ref[0:hist, :] = xr_ref[tt - hist:tt, :]

    xcb = xc.astype(BF16)
    nchunk = d // GATE_CHUNK
    gates = [jnp.dot(xcb[:, c * GATE_CHUNK:(c + 1) * GATE_CHUNK], wg_ref[c], preferred_element_type=F32)
             for c in range(nchunk)]
    r = jax.nn.sigmoid(jnp.concatenate([g[:, :GATE_CHUNK] for g in gates], axis=1) + bg_ref[0:1, :])
    ig = jax.nn.sigmoid(jnp.concatenate([g[:, GATE_CHUNK:] for g in gates], axis=1) + bg_ref[1:2, :])
    log_a = (-RG_C) * r * jax.nn.softplus(-lam_ref[...])
    a = jnp.exp(log_a)
    a_ref[...] = a
    one_minus_a2 = jnp.tanh(-log_a) * (a * a + 1.0)
    root = jnp.where(one_minus_a2 > 0.0, one_minus_a2 * lax.rsqrt(one_minus_a2), 0.0)
    b_ref[...] = root * (ig * xc)

    sub = lax.broadcasted_iota(jnp.int32, (8, d), 0)

    def group(g, hprev):
        rows = pl.ds(pl.multiple_of(g * 8, 8), 8)
        a = a_ref[rows, :]
        b = b_ref[rows, :]
        for sh in (1, 2, 4):
            keep = sub >= sh
            a_sh = jnp.where(keep, pltpu.roll(a, sh, 0), 1.0)
            b_sh = jnp.where(keep, pltpu.roll(b, sh, 0), 0.0)
            b = a * b_sh + b
            a = a * a_sh
        h = a * hprev + b
        b_ref[rows, :] = h
        return h[7:8, :]

    hlast = lax.fori_loop(0, tt // 8, group, h_ref[...])
    h_ref[...] = hlast
    hlast_ref[...] = hlast
    y_ref[...] = (b_ref[...] * jax.nn.gelu(xg_ref[...])).astype(y_ref.dtype)


def _pack_gate_weights(w_a, w_x):
    nb, blk, _ = w_a.shape
    per = GATE_CHUNK // blk

    def bd(w):
        w = w.reshape(nb // per, per, blk, blk)
        eye = jnp.eye(per, dtype=w.dtype)
        return jnp.einsum("cpij,pq->cpiqj", w, eye).reshape(nb // per, GATE_CHUNK, GATE_CHUNK)

    return jnp.concatenate([bd(w_a), bd(w_x)], axis=2).astype(BF16)


def _rglru(xr, xg, conv0, h0, conv_w, conv_b, w_rg_a, b_rg_a, w_rg_x, b_rg_x, lam, tt):
    b, t, d = xr.shape
    assert t % tt == 0 and tt % 8 == 0 and t >= CONV_WIDTH - 1
    wg = _pack_gate_weights(w_rg_a, w_rg_x)
    bg = jnp.stack([b_rg_a.reshape(d), b_rg_x.reshape(d)])
    seq = pl.BlockSpec((None, tt, d), lambda bi, i: (bi, i, 0))
    y, hlast = pl.pallas_call(
        functools.partial(_rglru_kernel, tt=tt, d=d),
        grid=(b, t // tt),
        in_specs=[seq, seq,
                  pl.BlockSpec((None, CONV_WIDTH - 1, d), lambda bi, i: (bi, 0, 0)),
                  pl.BlockSpec((None, 1, d), lambda bi, i: (bi, 0, 0)),
                  _resident((CONV_WIDTH, d)), _resident((1, d)), _resident(wg.shape),
                  _resident((2, d)), _resident((1, d))],
        out_specs=[seq, pl.BlockSpec((None, 1, d), lambda bi, i: (bi, 0, 0))],
        out_shape=[jax.ShapeDtypeStruct((b, t, d), BF16), jax.ShapeDtypeStruct((b, 1, d), F32)],
        scratch_shapes=[pltpu.VMEM((CONV_WIDTH - 1, d), F32), pltpu.VMEM((tt, d), F32),
                        pltpu.VMEM((tt, d), F32), pltpu.VMEM((1, d), F32)],
        compiler_params=_params(("parallel", "arbitrary")),
        name="rglru",
    )(xr, xg, conv0, h0.reshape(b, 1, d), conv_w, conv_b.reshape(1, d), wg, bg, lam.reshape(1, d))
    return y, hlast.reshape(b, d)


def _layer_norm(x, g, b):
    mu = jnp.mean(x, axis=-1, keepdims=True)
    xc = x - mu
    var = jnp.mean(xc * xc, axis=-1, keepdims=True)
    return xc * lax.rsqrt(var + LN_EPS) * g + b


def _merge_ffn_kernel(x_ref, gl_ref, ysb_ref, yrg_ref, bm_ref, wsb_ref, wrg_ref, wo_ref, ln1_ref,
                      wgu_ref, wd_ref, ln2_ref, y_ref, f_ref, *, d, dff, alpha, fc):
    ya = jnp.dot(ysb_ref[...], wsb_ref[...], preferred_element_type=F32)
    yb = jnp.dot(yrg_ref[...], wrg_ref[...], preferred_element_type=F32)
    g = jax.nn.sigmoid(gl_ref[...] + bm_ref[...])
    m = (g[:, :d] * ya + g[:, d:] * yb).astype(BF16)
    h1 = _layer_norm(alpha * x_ref[...] + jnp.dot(m, wo_ref[...], preferred_element_type=F32),
                     ln1_ref[0:1, :], ln1_ref[1:2, :])
    h1b = h1.astype(BF16)
    for c in range(dff // fc):
        gate = jnp.dot(h1b, wgu_ref[0, :, c * fc:(c + 1) * fc], preferred_element_type=F32)
        up = jnp.dot(h1b, wgu_ref[1, :, c * fc:(c + 1) * fc], preferred_element_type=F32)
        f_ref[:, c * fc:(c + 1) * fc] = (jax.nn.silu(gate) * up).astype(BF16)
    f = jnp.dot(f_ref[...], wd_ref[...], preferred_element_type=F32)
    y_ref[...] = _layer_norm(alpha * h1 + f, ln2_ref[0:1, :], ln2_ref[1:2, :])


def _merge_ffn(x2d, gl, y_sb, y_rg, b_merge, w_sb, w_rg, w_o, ln1, w_gu, w_down, ln2, alpha, tm):
    n, d = x2d.shape
    dff = w_down.shape[0]
    fc = 256
    assert n % tm == 0 and dff % fc == 0
    row = lambda w: pl.BlockSpec((tm, w), lambda i: (i, 0))
    return pl.pallas_call(
        functools.partial(_merge_ffn_kernel, d=d, dff=dff, alpha=alpha, fc=fc),
        grid=(n // tm,),
        in_specs=[row(d), row(N_BRANCHES * d), row(d), row(d),
                  _resident((1, N_BRANCHES * d)), _resident(w_sb.shape), _resident(w_rg.shape),
                  _resident(w_o.shape), _resident((2, d)), _resident(w_gu.shape),
                  _resident(w_down.shape), _resident((2, d))],
        out_specs=row(d),
        out_shape=jax.ShapeDtypeStruct((n, d), F32),
        scratch_shapes=[pltpu.VMEM((tm, dff), BF16)],
        compiler_params=_params(("parallel",)),
        name="merge_ffn",
    )(x2d, gl, y_sb, y_rg, b_merge.reshape(1, -1), w_sb, w_rg, w_o, ln1, w_gu, w_down, ln2)


def _layer(x, past_k, past_v, conv0, h0, wts, *, tm_proj, tqb, tq, tk, tt, tm_ffn):
    b, t, d = x.shape
    time_minor = t % LANES == 0
    outs = _in_proj(x if time_minor else x.reshape(1, b * t, d), wts["w_in"], tm_proj, time_minor)
    q, k, v, kb, vb, xr, xg, gl = (a.reshape((b, t) + a.shape[2:]) for a in outs)
    q_start = 0
    if past_k is not None:
        q_start = past_k.shape[1]
        kb = jnp.concatenate([past_k.reshape(b, q_start, d).astype(BF16), kb], axis=1)
        vb = jnp.concatenate([past_v.reshape(b, q_start, d).astype(BF16), vb], axis=1)
    y_sb = _attention(q, kb, vb, q_start, tqb, tq, min(tk, kb.shape[1]))
    y_rg, h_last = _rglru(xr, xg, conv0, h0, wts["conv_w"], wts["conv_b"], wts["w_rg_a"],
                          wts["b_rg_a"], wts["w_rg_x"], wts["b_rg_x"], wts["lam"], tt)
    flat = lambda a: a.reshape(b * t, a.shape[-1])
    y = _merge_ffn(flat(x), flat(gl), flat(y_sb), flat(y_rg), wts["b_merge"],
                   wts["w_sb_proj"], wts["w_rg_proj"], wts["w_o"], wts["ln1"], wts["w_gu"],
                   wts["w_down"], wts["ln2"], wts["alpha"], tm_ffn)
    conv_new = xr[:, t - (CONV_WIDTH - 1):, :]
    return y.reshape(b, t, d), k, v, conv_new, h_last


def kernel(x_prompt, x_sample, cache_k, cache_v, state_conv, state_h, w_in, b_merge, w_sb_proj, w_rg_proj, conv_w, conv_b, w_rg_a, b_rg_a, w_rg_x, b_rg_x, lam, w_o, ln1_g, ln1_b, w_gate, w_up, w_down, ln2_g, ln2_b):
    depth = 1
    wts = dict(
        w_in=w_in.astype(BF16), b_merge=b_merge, w_sb_proj=w_sb_proj.astype(BF16),
        w_rg_proj=w_rg_proj.astype(BF16), conv_w=conv_w, conv_b=conv_b, w_rg_a=w_rg_a, b_rg_a=b_rg_a,
        w_rg_x=w_rg_x, b_rg_x=b_rg_x, lam=lam, w_o=w_o.astype(BF16), ln1=jnp.stack([ln1_g, ln1_b]),
        w_gu=jnp.stack([w_gate, w_up]).astype(BF16), w_down=w_down.astype(BF16),
        ln2=jnp.stack([ln2_g, ln2_b]), alpha=(2.0 * depth) ** 0.25)

    bp, tp, d = x_prompt.shape
    zero_conv = jnp.zeros((bp, CONV_WIDTH - 1, d), x_prompt.dtype)
    zero_h = jnp.zeros((bp, d), x_prompt.dtype)
    y_p, k_p, v_p, conv_p, h_p = _layer(
        x_prompt, None, None, zero_conv, zero_h, wts,
        tm_proj=min(256, tp), tqb=min(256, tp), tq=min(64, tp), tk=256, tt=min(256, tp),
        tm_ffn=min(512, bp * tp))

    bs, ts, _ = x_sample.shape
    y_s, k_s, v_s, conv_s, h_s = _layer(
        x_sample, cache_k, cache_v, state_conv, state_h, wts,
        tm_proj=bs * ts, tqb=ts, tq=ts, tk=256, tt=ts, tm_ffn=bs * ts)

    return (y_p, y_s, k_p, v_p, conv_p, h_p, k_s, v_s, conv_s, h_s)
```

```python
import functools

import jax
import jax.numpy as jnp
from jax import lax
from jax.experimental import pallas as pl
from jax.experimental.pallas import tpu as pltpu

F32 = jnp.float32
BF16 = jnp.bfloat16

N_HEADS = 16
HEAD_DIM = 64
LANES = 128
SUBLANES = 8
HEADS_PER_SLAB = LANES // HEAD_DIM
SLABS = N_HEADS // HEADS_PER_SLAB
SLABS_PER_GROUP = 4
KV_CHUNK = 16
CONV_WIDTH = 4
RG_C = 8.0
LN_EPS = 1e-5
N_BRANCHES = 2
GATE_CHUNK = 256
SCAN_ROWS = 32
FF_CHUNK = 256
F32_EXP_UNDERFLOW = 104.0
VMEM_LIMIT = 56 * 1024 * 1024


def _params(sem):
    return pltpu.CompilerParams(dimension_semantics=sem, vmem_limit_bytes=VMEM_LIMIT)


def _resident(shape):
    nd = len(shape)
    return pl.BlockSpec(shape, lambda *_: (0,) * nd, pipeline_mode=pl.Buffered(1))


def _rglru_stages(xr_ref, xg_ref, cw_ref, cb_ref, wg_ref, bg_ref, lam_ref, hist_ref, xc_ref, g_ref, h_ref,
                  store_y, *, tt, d):
    hist = CONV_WIDTH - 1
    nchunk = d // GATE_CHUNK

    def conv_and_gates():
        x = xr_ref[...]
        first_row = lax.broadcasted_iota(jnp.int32, (tt, d), 0) == 0
        u = None
        for j in range(hist):
            u_prev = cw_ref[j:j + 1, :] * hist_ref[hist - 1:hist, :]
            for m in range(j):
                u_prev = u_prev + cw_ref[m:m + 1, :] * hist_ref[hist - 1 - (j - m):hist - (j - m), :]
            u = cw_ref[j:j + 1, :] * x if u is None else cw_ref[j:j + 1, :] * x + u
            u = jnp.where(first_row, u_prev, pltpu.roll(u, 1, 0))
        xc = cb_ref[...] + (cw_ref[hist:hist + 1, :] * x + u)
        hist_ref[...] = xr_ref[tt - hist:tt, :]
        xc_ref[...] = xc
        xcb = xc.astype(BF16)
        for c in range(nchunk):
            g_ref[:, 2 * c * GATE_CHUNK:2 * (c + 1) * GATE_CHUNK] = jnp.dot(
                xcb[:, c * GATE_CHUNK:(c + 1) * GATE_CHUNK], wg_ref[c], preferred_element_type=F32)

    n = min(SCAN_ROWS, tt)
    assert tt % n == 0 and n % SUBLANES == 0

    def recur(k):
        rows = slice(k * n, (k + 1) * n)
        sub = lax.broadcasted_iota(jnp.int32, (SUBLANES, GATE_CHUNK), 0)
        for c in range(nchunk):
            cols = slice(c * GATE_CHUNK, (c + 1) * GATE_CHUNK)
            pre = lambda half: g_ref[rows, (2 * c + half) * GATE_CHUNK:(2 * c + half + 1) * GATE_CHUNK]
            r = jax.nn.sigmoid(pre(0) + bg_ref[0:1, cols])
            ig = jax.nn.sigmoid(pre(1) + bg_ref[1:2, cols])
            log_a = (-RG_C) * r * jax.nn.softplus(-lam_ref[:, cols])
            a = jnp.exp(log_a)
            one_minus_a2 = jnp.tanh(-log_a) * (a * a + 1.0)
            root = jnp.where(one_minus_a2 > 0.0, one_minus_a2 * lax.rsqrt(one_minus_a2), 0.0)
            b = root * (ig * xc_ref[rows, cols])
            hprev = h_ref[:, cols]
            hs = []
            for g8 in range(n // SUBLANES):
                grp = slice(g8 * SUBLANES, (g8 + 1) * SUBLANES)
                ag, bgp = a[grp], b[grp]
                for sh in (1, 2, 4):
                    keep = sub >= sh
                    a_sh = jnp.where(keep, pltpu.roll(ag, sh, 0), 1.0)
                    b_sh = jnp.where(keep, pltpu.roll(bgp, sh, 0), 0.0)
                    bgp = ag * b_sh + bgp
                    ag = ag * a_sh
                h = ag * hprev + bgp
                hs.append(h)
                hprev = h[SUBLANES - 1:SUBLANES, :]
            h_ref[:, cols] = hprev
            store_y(rows, cols, (jnp.concatenate(hs, axis=0) * jax.nn.gelu(xg_ref[rows, cols])).astype(BF16))

    return [conv_and_gates] + [functools.partial(recur, k) for k in range(tt // n)]


def _rglru_kernel(xr_ref, xg_ref, conv0_ref, h0_ref, cw_ref, cb_ref, wg_ref, bg_ref, lam_ref,
                  y_ref, hlast_ref, hist_ref, xc_ref, g_ref, h_ref, *, tt, d):
    @pl.when(pl.program_id(1) == 0)
    def _():
        hist_ref[...] = conv0_ref[...]
        h_ref[...] = h0_ref[...]

    def store_y(rows, cols, y):
        y_ref[rows, cols] = y

    for stage in _rglru_stages(xr_ref, xg_ref, cw_ref, cb_ref, wg_ref, bg_ref, lam_ref, hist_ref, xc_ref,
                               g_ref, h_ref, store_y, tt=tt, d=d):
        stage()
    hlast_ref[...] = h_ref[...]


def _pack_gate_weights(w_a, w_x):
    nb, blk, _ = w_a.shape
    per = GATE_CHUNK // blk

    def bd(w):
        w = w.reshape(nb // per, per, blk, blk)
        eye = jnp.eye(per, dtype=w.dtype)
        return jnp.einsum("cpij,pq->cpiqj", w, eye).reshape(nb // per, GATE_CHUNK, GATE_CHUNK)

    return jnp.concatenate([bd(w_a), bd(w_x)], axis=2).astype(BF16)


def _rglru_operands(wts, d):
    wg = _pack_gate_weights(wts["w_rg_a"], wts["w_rg_x"])
    bg = jnp.stack([wts["b_rg_a"].reshape(d), wts["b_rg_x"].reshape(d)])
    return [wts["conv_w"], wts["conv_b"].reshape(1, d), wg, bg, wts["lam"].reshape(1, d)]


def _rglru_scratch(tt, d):
    return [pltpu.VMEM((CONV_WIDTH - 1, d), F32), pltpu.VMEM((tt, d), F32),
            pltpu.VMEM((tt, N_BRANCHES * d), F32), pltpu.VMEM((1, d), F32)]


def _rglru(xr, xg, conv0, h0, wts, tt):
    b, t, d = xr.shape
    assert t % tt == 0 and tt % SUBLANES == 0 and t >= CONV_WIDTH - 1
    consts = _rglru_operands(wts, d)
    seq = pl.BlockSpec((None, tt, d), lambda bi, i: (bi, i, 0))
    per_batch = lambda r: pl.BlockSpec((None, r, d), lambda bi, i: (bi, 0, 0))
    y, hlast = pl.pallas_call(
        functools.partial(_rglru_kernel, tt=tt, d=d),
        grid=(b, t // tt),
        in_specs=[seq, seq, per_batch(CONV_WIDTH - 1), per_batch(1)] + [_resident(c.shape) for c in consts],
        out_specs=[seq, per_batch(1)],
        out_shape=[jax.ShapeDtypeStruct((b, t, d), BF16), jax.ShapeDtypeStruct((b, 1, d), F32)],
        scratch_shapes=_rglru_scratch(tt, d),
        compiler_params=_params(("parallel", "arbitrary")),
        name="rglru",
    )(xr, xg, conv0, h0.reshape(b, 1, d), *consts)
    return y, hlast.reshape(b, d)


PROJ_CHUNK = 256


def _project_stages(x_ref, w_ref, q_ref, k_ref, v_ref, kb_ref, vb_ref, xr_ref, xg_ref, gl_ref, *,
                    d, tm, time_minor):
    st = {}

    def load():
        st["xb"] = x_ref[...].astype(BF16)

    def mm(c0):
        return jnp.dot(st["xb"], w_ref[:, c0:c0 + PROJ_CHUNK], preferred_element_type=F32)

    def q_chunk(c0):
        q_ref[:, c0:c0 + PROJ_CHUNK] = (mm(c0) * (HEAD_DIM ** -0.5)).astype(BF16)

    def kv_chunk(o_ref, ob_ref, base, c0):
        u = mm(base + c0)
        ob_ref[:, c0:c0 + PROJ_CHUNK] = u.astype(BF16)
        if time_minor:
            o_ref[c0:c0 + PROJ_CHUNK, :] = u.T
        else:
            for j in range(PROJ_CHUNK // HEAD_DIM):
                h = c0 // HEAD_DIM + j
                o_ref[pl.ds(h, tm, stride=N_HEADS), :] = u[:, j * HEAD_DIM:(j + 1) * HEAD_DIM]

    def plain_chunk(o_ref, base, c0):
        o_ref[:, c0:c0 + PROJ_CHUNK] = mm(base + c0)

    cols = range(0, d, PROJ_CHUNK)
    cast = ([functools.partial(q_chunk, c0) for c0 in cols]
            + [functools.partial(kv_chunk, k_ref, kb_ref, d, c0) for c0 in cols]
            + [functools.partial(kv_chunk, v_ref, vb_ref, 2 * d, c0) for c0 in cols])
    plain = ([functools.partial(plain_chunk, xr_ref, 3 * d, c0) for c0 in cols]
             + [functools.partial(plain_chunk, xg_ref, 4 * d, c0) for c0 in cols]
             + [functools.partial(plain_chunk, gl_ref, 5 * d, c0) for c0 in range(0, 2 * d, PROJ_CHUNK)])
    return load, plain, cast


def _in_proj_kernel(*refs, d, tm):
    load, plain, cast = _project_stages(*refs, d=d, tm=tm, time_minor=False)
    load()
    for stage in cast + plain:
        stage()


def _in_proj(x2d, w_in_bf, tm):
    n, d = x2d.shape
    assert n % tm == 0
    row = lambda w: pl.BlockSpec((tm, w), lambda i: (i, 0))
    row_shape = lambda w, dt: jax.ShapeDtypeStruct((n, w), dt)
    kv_spec = pl.BlockSpec((tm * N_HEADS, HEAD_DIM), lambda i: (i, 0))
    kv_shape = jax.ShapeDtypeStruct((n * N_HEADS, HEAD_DIM), F32)
    q, k, v, kb, vb, xr, xg, gl = pl.pallas_call(
        functools.partial(_in_proj_kernel, d=d, tm=tm),
        grid=(n // tm,),
        in_specs=[row(d), _resident(w_in_bf.shape)],
        out_specs=[row(d), kv_spec, kv_spec, row(d), row(d), row(d), row(d), row(2 * d)],
        out_shape=[row_shape(d, BF16), kv_shape, kv_shape, row_shape(d, BF16), row_shape(d, BF16),
                   row_shape(d, F32), row_shape(d, F32), row_shape(2 * d, F32)],
        compiler_params=_params(("parallel",)),
        name="in_proj",
    )(x2d, w_in_bf)
    k, v = (a.reshape(n, N_HEADS, HEAD_DIM) for a in (k, v))
    return q, k, v, kb, vb, xr, xg, gl


def _in_proj_rglru_kernel(x_ref, w_ref, conv0_ref, h0_ref, cw_ref, cb_ref, wg_ref, bg_ref, lam_ref,
                          q_ref, k_ref, v_ref, kb_ref, vb_ref, gl_ref, yrg_ref, conv_ref, hlast_ref,
                          xr_ref, xg_ref, hist_ref, xc_ref, g_ref, h_ref, *, d, tm, n_tiles):
    s = pl.program_id(0)
    prev = jnp.maximum(s - 1, 0)

    @pl.when(s == 0)
    def _():
        xr_ref[...] = jnp.zeros_like(xr_ref)
        xg_ref[...] = jnp.zeros_like(xg_ref)

    @pl.when(prev % n_tiles == 0)
    def _():
        hist_ref[...] = conv0_ref[...]
        h_ref[...] = h0_ref[...]

    def store_y(rows, cols, y):
        yrg_ref[rows, cols] = y

    load, plain, cast = _project_stages(x_ref, w_ref, q_ref, k_ref, v_ref, kb_ref, vb_ref, xr_ref, xg_ref,
                                        gl_ref, d=d, tm=tm, time_minor=True)
    load()
    for stage in _rglru_stages(xr_ref, xg_ref, cw_ref, cb_ref, wg_ref, bg_ref, lam_ref, hist_ref, xc_ref,
                               g_ref, h_ref, store_y, tt=tm, d=d):
        stage()
    for stage in cast + plain:
        stage()

    @pl.when(prev % n_tiles == n_tiles - 1)
    def _():
        conv_ref[...] = hist_ref[...]
        hlast_ref[...] = h_ref[...]


def _in_proj_rglru(x, conv0, h0, wts, tm):
    b, t, d = x.shape
    n_tiles = t // tm
    n = b * n_tiles
    assert t % tm == 0 and tm % LANES == 0 and t >= CONV_WIDTH - 1
    consts = _rglru_operands(wts, d)
    cur = lambda s: jnp.minimum(s, n - 1)
    prev = lambda s: jnp.maximum(s - 1, 0)
    row = lambda w: pl.BlockSpec((tm, w), lambda s: (cur(s), 0))
    row_shape = lambda w, dt: jax.ShapeDtypeStruct((b * t, w), dt)
    kv_spec = pl.BlockSpec((None, d, tm), lambda s: (cur(s) // n_tiles, 0, cur(s) % n_tiles))
    kv_shape = jax.ShapeDtypeStruct((b, d, t), F32)
    per_batch = lambda r: pl.BlockSpec((None, r, d), lambda s: (prev(s) // n_tiles, 0, 0))
    q, k, v, kb, vb, gl, y_rg, conv_new, hlast = pl.pallas_call(
        functools.partial(_in_proj_rglru_kernel, d=d, tm=tm, n_tiles=n_tiles),
        grid=(n + 1,),
        in_specs=[row(d), _resident(wts["w_in"].shape), per_batch(CONV_WIDTH - 1), per_batch(1)]
                 + [_resident(c.shape) for c in consts],
        out_specs=[row(d), kv_spec, kv_spec, row(d), row(d), row(2 * d),
                   pl.BlockSpec((tm, d), lambda s: (prev(s), 0)), per_batch(CONV_WIDTH - 1), per_batch(1)],
        out_shape=[row_shape(d, BF16), kv_shape, kv_shape, row_shape(d, BF16), row_shape(d, BF16),
                   row_shape(2 * d, F32), row_shape(d, BF16),
                   jax.ShapeDtypeStruct((b, CONV_WIDTH - 1, d), F32), jax.ShapeDtypeStruct((b, 1, d), F32)],
        scratch_shapes=[pltpu.VMEM((tm, d), F32)] * 2 + _rglru_scratch(tm, d),
        compiler_params=_params(("arbitrary",)),
        name="in_proj_rglru",
    )(x.reshape(b * t, d), wts["w_in"], conv0, h0.reshape(b, 1, d), *consts)
    k, v = (a.reshape(b, N_HEADS, HEAD_DIM, t).transpose(0, 3, 1, 2) for a in (k, v))
    return q, k, v, kb, vb, gl, y_rg, conv_new, hlast.reshape(b, d)


def _attn_subtile(q_ref, k_ref, v_ref, tri, acc_ref, carry_ref, z_ref, hl_ref, *, j, w, qs_blk, tq, tk, first):
    row0 = pl.multiple_of(j * tq, tq)
    rows = pl.ds(row0, tq)
    qs = qs_blk + j * tq
    hi_key = jnp.maximum(qs + tq - w * tk, 0)
    s = jnp.maximum(qs + tq - (w + 1) * tk, 0)
    win = pl.ds(s // KV_CHUNK, tk // KV_CHUNK)
    row_iota = lax.broadcasted_iota(jnp.int32, (tq, 1), 0)
    col_iota = lax.broadcasted_iota(jnp.int32, (tq, tk), 1)
    mask = col_iota < (jnp.minimum(qs + row_iota, hi_key) - s)
    mask2 = jnp.concatenate([mask, mask], axis=0)
    first_head = lax.broadcasted_iota(jnp.int32, (tq, LANES), 1) < HEAD_DIM
    t2 = HEADS_PER_SLAB * tq
    grp_rows = SLABS_PER_GROUP * t2

    later = []
    for grp in range(SLABS // SLABS_PER_GROUP):
        for sl in range(SLABS_PER_GROUP):
            slab = grp * SLABS_PER_GROUP + sl
            lanes = slice(slab * LANES, (slab + 1) * LANES)
            q2 = q_ref[rows, lanes]
            zero = jnp.zeros_like(q2)
            qq = jnp.concatenate([jnp.where(first_head, q2, zero), jnp.where(first_head, zero, q2)], axis=0)
            k2 = k_ref[win, :, lanes].reshape(tk, LANES)
            z = lax.dot_general(qq, k2, (((1,), (1,)), ((), ())), preferred_element_type=F32)
            neg_abs_z = pltpu.bitcast(pltpu.bitcast(z, jnp.uint32) | jnp.uint32(0x80000000), F32)
            sp = jnp.where(mask2, jnp.maximum(z, 0.0) + jnp.log(1.0 + jnp.exp(neg_abs_z)), 0.0)
            sp_hi = pltpu.bitcast(pltpu.bitcast(sp, jnp.uint32) & jnp.uint32(0xFFFF0000), F32)
            srows = slice(slab * t2, (slab + 1) * t2)
            z_ref[srows, :] = z
            hl_ref[srows, 0:tk] = sp_hi
            hl_ref[srows, tk:2 * tk] = sp - sp_hi
        later.append(lax.dot_general(hl_ref[grp * grp_rows:(grp + 1) * grp_rows, :], tri,
                                     (((1,), (0,)), ((), ())), preferred_element_type=F32))

    cmin = jnp.full((tq, 1), jnp.inf, F32)
    for slab in range(SLABS):
        lanes = slice(slab * LANES, (slab + 1) * LANES)
        sl = slab % SLABS_PER_GROUP
        lat = later[slab // SLABS_PER_GROUP][sl * t2:(sl + 1) * t2, :]
        x = z_ref[slab * t2:(slab + 1) * t2, :] - lat
        tot = lat[:, 0:1]
        ha, hb = slab * HEADS_PER_SLAB, slab * HEADS_PER_SLAB + 1
        if not first:
            c_old = jnp.concatenate([carry_ref[rows, ha:ha + 1], carry_ref[rows, hb:hb + 1]], axis=0)
            x = x - c_old
            tot = tot + c_old
        wgt = jnp.where(mask2, jnp.exp(x), 0.0).astype(BF16)
        o2 = jnp.dot(wgt, v_ref[win, :, lanes].reshape(tk, LANES), preferred_element_type=F32)
        o = jnp.where(first_head, o2[:tq], o2[tq:])
        if first:
            acc_ref[rows, lanes] = o
        else:
            acc_ref[rows, lanes] += o
        carry_ref[rows, ha:ha + 1] = tot[:tq]
        carry_ref[rows, hb:hb + 1] = tot[tq:]
        cmin = jnp.minimum(cmin, jnp.minimum(tot[:tq], tot[tq:]))
    return cmin


def _attn_kernel(q_ref, k_ref, v_ref, tri_ref, o_ref, acc_ref, carry_ref, z_ref, hl_ref, *, q_start, tqb, tq, tk):
    qs_blk = q_start + pl.program_id(1) * tqb
    tri = tri_ref[...]
    sub = functools.partial(_attn_subtile, q_ref, k_ref, v_ref, tri, acc_ref, carry_ref, z_ref, hl_ref,
                            qs_blk=qs_blk, tq=tq, tk=tk)
    inf = jnp.full((tq, 1), jnp.inf, F32)

    def more_keys(w, cmin):
        return jnp.logical_and(jnp.min(cmin) < F32_EXP_UNDERFLOW, qs_blk + tqb - (w + 1) * tk > 0)

    cmin = lax.fori_loop(0, tqb // tq, lambda j, c: jnp.minimum(c, sub(j=j, w=0, first=True)), inf)

    def older_window(state):
        w, _ = state
        cmin = lax.fori_loop(0, tqb // tq, lambda j, c: jnp.minimum(c, sub(j=j, w=w, first=False)), inf)
        return w + 1, more_keys(w, cmin)

    lax.while_loop(lambda st: st[1], older_window, (jnp.int32(1), more_keys(0, cmin)))
    o_ref[...] = acc_ref[...].astype(o_ref.dtype)


def _tri(tk):
    j = lax.broadcasted_iota(jnp.int32, (tk, tk), 0)
    s = lax.broadcasted_iota(jnp.int32, (tk, tk), 1)
    m = (j >= s).astype(BF16)
    return jnp.concatenate([m, m], axis=0)


def _attention(q, k, v, q_start, tqb, tq, tk):
    b, t_q, d = q.shape
    t_k = k.shape[1]
    assert d == N_HEADS * HEAD_DIM and t_k == q_start + t_q and t_k >= tk
    assert t_q % tqb == 0 and tqb % tq == 0 and tq % 16 == 0 and tk % 16 == 0 and q_start % 16 == 0
    assert t_k % KV_CHUNK == 0 and tk % KV_CHUNK == 0
    k, v = (a.reshape(b, t_k // KV_CHUNK, KV_CHUNK, d) for a in (k, v))
    kv_spec = pl.BlockSpec((None, t_k // KV_CHUNK, KV_CHUNK, d), lambda bi, i: (bi, 0, 0, 0),
                           pipeline_mode=pl.Buffered(1))
    return pl.pallas_call(
        functools.partial(_attn_kernel, q_start=q_start, tqb=tqb, tq=tq, tk=tk),
        grid=(b, t_q // tqb),
        in_specs=[pl.BlockSpec((None, tqb, d), lambda bi, i: (bi, i, 0)), kv_spec, kv_spec,
                  _resident((2 * tk, tk))],
        out_specs=pl.BlockSpec((None, tqb, d), lambda bi, i: (bi, i, 0)),
        out_shape=jax.ShapeDtypeStruct((b, t_q, d), BF16),
        scratch_shapes=[pltpu.VMEM((tqb, d), F32), pltpu.VMEM((tqb, N_HEADS), F32),
                        pltpu.VMEM((N_HEADS * tq, tk), F32), pltpu.VMEM((N_HEADS * tq, 2 * tk), F32)],
        compiler_params=_params(("parallel", "arbitrary")),
        name="stickbreak_attn",
    )(q, k, v, _tri(tk))


def _layer_norm(x, g, b):
    mu = jnp.mean(x, axis=-1, keepdims=True)
    xc = x - mu
    var = jnp.mean(xc * xc, axis=-1, keepdims=True)
    return xc * lax.rsqrt(var + LN_EPS) * g + b


def _merge_ffn_kernel(x_ref, gl_ref, ysb_ref, yrg_ref, bm_ref, wsb_ref, wrg_ref, wo_ref, ln1_ref,
                      wgu_ref, wd_ref, ln2_ref, y_ref, f_ref, *, d, dff, alpha):
    ya = jnp.dot(ysb_ref[...], wsb_ref[...], preferred_element_type=F32)
    yb = jnp.dot(yrg_ref[...], wrg_ref[...], preferred_element_type=F32)
    g = jax.nn.sigmoid(gl_ref[...] + bm_ref[...])
    m = (g[:, :d] * ya + g[:, d:] * yb).astype(BF16)
    h1 = _layer_norm(alpha * x_ref[...] + jnp.dot(m, wo_ref[...], preferred_element_type=F32),
                     ln1_ref[0:1, :], ln1_ref[1:2, :])
    h1b = h1.astype(BF16)
    for c in range(dff // FF_CHUNK):
        cols = slice(c * FF_CHUNK, (c + 1) * FF_CHUNK)
        gate = jnp.dot(h1b, wgu_ref[0, :, cols], preferred_element_type=F32)
        up = jnp.dot(h1b, wgu_ref[1, :, cols], preferred_element_type=F32)
        f_ref[:, cols] = (jax.nn.silu(gate) * up).astype(BF16)
    f = jnp.dot(f_ref[...], wd_ref[...], preferred_element_type=F32)
    y_ref[...] = _layer_norm(alpha * h1 + f, ln2_ref[0:1, :], ln2_ref[1:2, :])


def _merge_ffn(x2d, gl, y_sb, y_rg, wts, tm):
    n, d = x2d.shape
    dff = wts["w_down"].shape[0]
    assert n % tm == 0 and dff % FF_CHUNK == 0
    consts = [wts["b_merge"].reshape(1, -1), wts["w_sb_proj"], wts["w_rg_proj"], wts["w_o"], wts["ln1"],
              wts["w_gu"], wts["w_down"], wts["ln2"]]
    row = lambda w: pl.BlockSpec((tm, w), lambda i: (i, 0))
    return pl.pallas_call(
        functools.partial(_merge_ffn_kernel, d=d, dff=dff, alpha=wts["alpha"]),
        grid=(n // tm,),
        in_specs=[row(d), row(N_BRANCHES * d), row(d), row(d)] + [_resident(c.shape) for c in consts],
        out_specs=row(d),
        out_shape=jax.ShapeDtypeStruct((n, d), F32),
        scratch_shapes=[pltpu.VMEM((tm, dff), BF16)],
        compiler_params=_params(("parallel",)),
        name="merge_ffn",
    )(x2d, gl, y_sb, y_rg, *consts)


def _layer(x, past_k, past_v, conv0, h0, wts, *, tm_proj, tqb, tq, tk, tm_ffn):
    b, t, d = x.shape
    n = b * t
    if past_k is None:
        q, k, v, kb, vb, gl, y_rg, conv_new, h_last = _in_proj_rglru(x, conv0, h0, wts, tm_proj)
        q_start = 0
    else:
        q, k, v, kb, vb, xr, xg, gl = _in_proj(x.reshape(n, d), wts["w_in"], tm_proj)
        k, v = (a.reshape(b, t, N_HEADS, HEAD_DIM) for a in (k, v))
        xr = xr.reshape(b, t, d)
        y_rg, h_last = _rglru(xr, xg.reshape(b, t, d), conv0, h0, wts, t)
        conv_new = xr[:, t - (CONV_WIDTH - 1):, :]
        q_start = past_k.shape[1]
        kb = jnp.concatenate([past_k.reshape(b, q_start, d).astype(BF16), kb.reshape(b, t, d)], axis=1)
        vb = jnp.concatenate([past_v.reshape(b, q_start, d).astype(BF16), vb.reshape(b, t, d)], axis=1)
    seq = lambda a: a.reshape(b, -1, d)
    y_sb = _attention(seq(q), seq(kb), seq(vb), q_start, tqb, tq, min(tk, q_start + t))
    y = _merge_ffn(x.reshape(n, d), gl.reshape(n, -1), y_sb.reshape(n, d), y_rg.reshape(n, d), wts, tm_ffn)
    return y.reshape(b, t, d), k, v, conv_new, h_last


def kernel(x_prompt, x_sample, cache_k, cache_v, state_conv, state_h, w_in, b_merge, w_sb_proj, w_rg_proj, conv_w, conv_b, w_rg_a, b_rg_a, w_rg_x, b_rg_x, lam, w_o, ln1_g, ln1_b, w_gate, w_up, w_down, ln2_g, ln2_b):
    depth = 1
    wts = dict(
        w_in=w_in.astype(BF16), b_merge=b_merge, w_sb_proj=w_sb_proj.astype(BF16),
        w_rg_proj=w_rg_proj.astype(BF16), conv_w=conv_w, conv_b=conv_b, w_rg_a=w_rg_a, b_rg_a=b_rg_a,
        w_rg_x=w_rg_x, b_rg_x=b_rg_x, lam=lam, w_o=w_o.astype(BF16), ln1=jnp.stack([ln1_g, ln1_b]),
        w_gu=jnp.stack([w_gate, w_up]).astype(BF16), w_down=w_down.astype(BF16),
        ln2=jnp.stack([ln2_g, ln2_b]), alpha=(2.0 * depth) ** 0.25)

    bp, tp, d = x_prompt.shape
    zero_conv = jnp.zeros((bp, CONV_WIDTH - 1, d), x_prompt.dtype)
    zero_h = jnp.zeros((bp, d), x_prompt.dtype)
    y_p, k_p, v_p, conv_p, h_p = _layer(
        x_prompt, None, None, zero_conv, zero_h, wts,
        tm_proj=min(256, tp), tqb=min(256, tp), tq=min(64, tp), tk=256, tm_ffn=min(512, bp * tp))

    bs, ts, _ = x_sample.shape
    y_s, k_s, v_s, conv_s, h_s = _layer(
        x_sample, cache_k, cache_v, state_conv, state_h, wts,
        tm_proj=bs * ts, tqb=ts, tq=ts, tk=256, tm_ffn=bs * ts)

    return (y_p, y_s, k_p, v_p, conv_p, h_p, k_s, v_s, conv_s, h_s)
```

```python
import functools

import jax
import jax.numpy as jnp
from jax import lax
from jax.experimental import pallas as pl
from jax.experimental.pallas import tpu as pltpu

F32 = jnp.float32
BF16 = jnp.bfloat16

N_HEADS = 16
HEAD_DIM = 64
LANES = 128
SUBLANES = 8
HEADS_PER_SLAB = LANES // HEAD_DIM
SLABS = N_HEADS // HEADS_PER_SLAB
SLABS_PER_GROUP = 4
KV_CHUNK = 16
CONV_WIDTH = 4
RG_C = 8.0
LN_EPS = 1e-5
N_BRANCHES = 2
GATE_CHUNK = 256
SCAN_ROWS = 32
FF_CHUNK = 256
F32_EXP_UNDERFLOW = 104.0
VMEM_LIMIT = 56 * 1024 * 1024


def _params(sem):
    return pltpu.CompilerParams(dimension_semantics=sem, vmem_limit_bytes=VMEM_LIMIT)


def _resident(shape):
    nd = len(shape)
    return pl.BlockSpec(shape, lambda *_: (0,) * nd, pipeline_mode=pl.Buffered(1))


def _rglru_stages(xr_ref, xg_ref, cw_ref, cb_ref, wg_ref, bg_ref, lam_ref, hist_ref, xc_ref, g_ref, h_ref,
                  store_y, *, tt, d):
    hist = CONV_WIDTH - 1
    nchunk = d // GATE_CHUNK

    def conv_and_gates():
        x = xr_ref[...]
        first_row = lax.broadcasted_iota(jnp.int32, (SUBLANES, d), 0) == 0
        u = None
        for j in range(hist):
            u_prev = cw_ref[j:j + 1, :] * hist_ref[hist - 1:hist, :]
            for m in range(j):
                u_prev = u_prev + cw_ref[m:m + 1, :] * hist_ref[hist - 1 - (j - m):hist - (j - m), :]
            u = cw_ref[j:j + 1, :] * x if u is None else cw_ref[j:j + 1, :] * x + u
            u = pltpu.roll(u, 1, 0)
            u = jnp.concatenate([jnp.where(first_row, u_prev, u[:SUBLANES]), u[SUBLANES:]], axis=0)
        xc = cb_ref[...] + (cw_ref[hist:hist + 1, :] * x + u)
        hist_ref[...] = xr_ref[tt - hist:tt, :]
        xc_ref[...] = xc
        xcb = xc.astype(BF16)
        for c in range(nchunk):
            g_ref[:, 2 * c * GATE_CHUNK:2 * (c + 1) * GATE_CHUNK] = jnp.dot(
                xcb[:, c * GATE_CHUNK:(c + 1) * GATE_CHUNK], wg_ref[c], preferred_element_type=F32)

    n = min(SCAN_ROWS, tt)
    assert tt % n == 0 and n % SUBLANES == 0

    def recur(k):
        rows = slice(k * n, (k + 1) * n)
        sub = lax.broadcasted_iota(jnp.int32, (SUBLANES, GATE_CHUNK), 0)
        for c in range(nchunk):
            cols = slice(c * GATE_CHUNK, (c + 1) * GATE_CHUNK)
            pre = lambda half: g_ref[rows, (2 * c + half) * GATE_CHUNK:(2 * c + half + 1) * GATE_CHUNK]
            r = jax.nn.sigmoid(pre(0) + bg_ref[0:1, cols])
            ig = jax.nn.sigmoid(pre(1) + bg_ref[1:2, cols])
            log_a = r * ((-RG_C) * jax.nn.softplus(-lam_ref[:, cols]))
            a = jnp.exp(log_a)
            one_minus_a2 = jnp.tanh(-log_a) * (a * a + 1.0)
            root = jnp.where(one_minus_a2 > 0.0, one_minus_a2 * lax.rsqrt(one_minus_a2), 0.0)
            b = root * (ig * xc_ref[rows, cols])
            hprev = h_ref[:, cols]
            hs = []
            for g8 in range(n // SUBLANES):
                grp = slice(g8 * SUBLANES, (g8 + 1) * SUBLANES)
                ag, bgp = a[grp], b[grp]
                for sh in (1, 2, 4):
                    keep = sub >= sh
                    a_sh = jnp.where(keep, pltpu.roll(ag, sh, 0), 1.0)
                    b_sh = jnp.where(keep, pltpu.roll(bgp, sh, 0), 0.0)
                    bgp = ag * b_sh + bgp
                    ag = ag * a_sh
                h = ag * hprev + bgp
                hs.append(h)
                hprev = h[SUBLANES - 1:SUBLANES, :]
            h_ref[:, cols] = hprev
            store_y(rows, cols, (jnp.concatenate(hs, axis=0) * jax.nn.gelu(xg_ref[rows, cols])).astype(BF16))

    return [conv_and_gates] + [functools.partial(recur, k) for k in range(tt // n)]


def _rglru_kernel(xr_ref, xg_ref, conv0_ref, h0_ref, cw_ref, cb_ref, wg_ref, bg_ref, lam_ref,
                  y_ref, hlast_ref, hist_ref, xc_ref, g_ref, h_ref, *, tt, d):
    @pl.when(pl.program_id(1) == 0)
    def _():
        hist_ref[...] = conv0_ref[...]
        h_ref[...] = h0_ref[...]

    def store_y(rows, cols, y):
        y_ref[rows, cols] = y

    for stage in _rglru_stages(xr_ref, xg_ref, cw_ref, cb_ref, wg_ref, bg_ref, lam_ref, hist_ref, xc_ref,
                               g_ref, h_ref, store_y, tt=tt, d=d):
        stage()
    hlast_ref[...] = h_ref[...]


def _pack_gate_weights(w_a, w_x):
    nb, blk, _ = w_a.shape
    per = GATE_CHUNK // blk

    def bd(w):
        w = w.reshape(nb // per, per, blk, blk)
        eye = jnp.eye(per, dtype=w.dtype)
        return jnp.einsum("cpij,pq->cpiqj", w, eye).reshape(nb // per, GATE_CHUNK, GATE_CHUNK)

    return jnp.concatenate([bd(w_a), bd(w_x)], axis=2).astype(BF16)


def _rglru_operands(wts, d):
    wg = _pack_gate_weights(wts["w_rg_a"], wts["w_rg_x"])
    bg = jnp.stack([wts["b_rg_a"].reshape(d), wts["b_rg_x"].reshape(d)])
    return [wts["conv_w"], wts["conv_b"].reshape(1, d), wg, bg, wts["lam"].reshape(1, d)]


def _rglru_scratch(tt, d):
    return [pltpu.VMEM((CONV_WIDTH - 1, d), F32), pltpu.VMEM((tt, d), F32),
            pltpu.VMEM((tt, N_BRANCHES * d), F32), pltpu.VMEM((1, d), F32)]


def _rglru(xr, xg, conv0, h0, wts, tt):
    b, t, d = xr.shape
    assert t % tt == 0 and tt % SUBLANES == 0 and t >= CONV_WIDTH - 1
    consts = _rglru_operands(wts, d)
    seq = pl.BlockSpec((None, tt, d), lambda bi, i: (bi, i, 0))
    per_batch = lambda r: pl.BlockSpec((None, r, d), lambda bi, i: (bi, 0, 0))
    y, hlast = pl.pallas_call(
        functools.partial(_rglru_kernel, tt=tt, d=d),
        grid=(b, t // tt),
        in_specs=[seq, seq, per_batch(CONV_WIDTH - 1), per_batch(1)] + [_resident(c.shape) for c in consts],
        out_specs=[seq, per_batch(1)],
        out_shape=[jax.ShapeDtypeStruct((b, t, d), BF16), jax.ShapeDtypeStruct((b, 1, d), F32)],
        scratch_shapes=_rglru_scratch(tt, d),
        compiler_params=_params(("parallel", "arbitrary")),
        name="rglru",
    )(xr, xg, conv0, h0.reshape(b, 1, d), *consts)
    return y, hlast.reshape(b, d)


PROJ_CHUNK = 256


def _project_stages(x_ref, w_ref, q_ref, k_ref, v_ref, kb_ref, vb_ref, xr_ref, xg_ref, gl_ref, xb_ref, *,
                    d, tm, time_minor):
    def load():
        xb_ref[...] = x_ref[...].astype(BF16)

    def mm(c0):
        return jnp.dot(xb_ref[...], w_ref[:, c0:c0 + PROJ_CHUNK], preferred_element_type=F32)

    def q_chunk(c0):
        q_ref[:, c0:c0 + PROJ_CHUNK] = (mm(c0) * (HEAD_DIM ** -0.5)).astype(BF16)

    def kv_chunk(o_ref, ob_ref, base, c0):
        u = mm(base + c0)
        ob_ref[:, c0:c0 + PROJ_CHUNK] = u.astype(BF16)
        if time_minor:
            o_ref[c0:c0 + PROJ_CHUNK, :] = u.T
        else:
            for j in range(PROJ_CHUNK // HEAD_DIM):
                h = c0 // HEAD_DIM + j
                o_ref[pl.ds(h, tm, stride=N_HEADS), :] = u[:, j * HEAD_DIM:(j + 1) * HEAD_DIM]

    def plain_chunk(o_ref, base, c0):
        o_ref[:, c0:c0 + PROJ_CHUNK] = mm(base + c0)

    cols = range(0, d, PROJ_CHUNK)
    cast = ([functools.partial(q_chunk, c0) for c0 in cols]
            + [functools.partial(kv_chunk, k_ref, kb_ref, d, c0) for c0 in cols]
            + [functools.partial(kv_chunk, v_ref, vb_ref, 2 * d, c0) for c0 in cols])
    gl = [functools.partial(plain_chunk, gl_ref, 5 * d, c0) for c0 in range(0, 2 * d, PROJ_CHUNK)]
    xr = [functools.partial(plain_chunk, xr_ref, 3 * d, c0) for c0 in cols]
    xg = [functools.partial(plain_chunk, xg_ref, 4 * d, c0) for c0 in cols]
    return load, cast + gl, xr, xg


def _in_proj_kernel(*refs, d, tm):
    load, qkv_gl, xr, xg = _project_stages(*refs, d=d, tm=tm, time_minor=False)
    load()
    for stage in qkv_gl + xr + xg:
        stage()


def _in_proj(x2d, w_in_bf, tm):
    n, d = x2d.shape
    assert n % tm == 0
    row = lambda w: pl.BlockSpec((tm, w), lambda i: (i, 0))
    row_shape = lambda w, dt: jax.ShapeDtypeStruct((n, w), dt)
    kv_spec = pl.BlockSpec((tm * N_HEADS, HEAD_DIM), lambda i: (i, 0))
    kv_shape = jax.ShapeDtypeStruct((n * N_HEADS, HEAD_DIM), F32)
    q, k, v, kb, vb, xr, xg, gl = pl.pallas_call(
        functools.partial(_in_proj_kernel, d=d, tm=tm),
        grid=(n // tm,),
        in_specs=[row(d), _resident(w_in_bf.shape)],
        out_specs=[row(d), kv_spec, kv_spec, row(d), row(d), row(d), row(d), row(2 * d)],
        out_shape=[row_shape(d, BF16), kv_shape, kv_shape, row_shape(d, BF16), row_shape(d, BF16),
                   row_shape(d, F32), row_shape(d, F32), row_shape(2 * d, F32)],
        scratch_shapes=[pltpu.VMEM((tm, d), BF16)],
        compiler_params=_params(("parallel",)),
        name="in_proj",
    )(x2d, w_in_bf)
    k, v = (a.reshape(n, N_HEADS, HEAD_DIM) for a in (k, v))
    return q, k, v, kb, vb, xr, xg, gl


def _in_proj_rglru_kernel(x_ref, w_ref, conv0_ref, h0_ref, cw_ref, cb_ref, wg_ref, bg_ref, lam_ref,
                          q_ref, k_ref, v_ref, kb_ref, vb_ref, gl_ref, yrg_ref, conv_ref, hlast_ref,
                          xr_ref, xg_ref, xb_ref, hist_ref, xc_ref, g_ref, h_ref, *, d, tm, n_tiles):
    s = pl.program_id(0)
    prev = jnp.maximum(s - 1, 0)

    @pl.when(s == 0)
    def _():
        xr_ref[...] = jnp.zeros_like(xr_ref)
        xg_ref[...] = jnp.zeros_like(xg_ref)

    @pl.when(prev % n_tiles == 0)
    def _():
        hist_ref[...] = conv0_ref[...]
        h_ref[...] = h0_ref[...]

    def store_y(rows, cols, y):
        yrg_ref[rows, cols] = y

    load, qkv_gl, xr, xg = _project_stages(x_ref, w_ref, q_ref, k_ref, v_ref, kb_ref, vb_ref, xr_ref, xg_ref,
                                           gl_ref, xb_ref, d=d, tm=tm, time_minor=True)
    rg = _rglru_stages(xr_ref, xg_ref, cw_ref, cb_ref, wg_ref, bg_ref, lam_ref, hist_ref, xc_ref,
                       g_ref, h_ref, store_y, tt=tm, d=d)
    load()
    for stage in rg + qkv_gl + xr + xg:
        stage()

    @pl.when(prev % n_tiles == n_tiles - 1)
    def _():
        conv_ref[...] = hist_ref[...]
        hlast_ref[...] = h_ref[...]


def _in_proj_rglru(x, conv0, h0, wts, tm):
    b, t, d = x.shape
    n_tiles = t // tm
    n = b * n_tiles
    assert t % tm == 0 and tm % LANES == 0 and t >= CONV_WIDTH - 1
    consts = _rglru_operands(wts, d)
    cur = lambda s: jnp.minimum(s, n - 1)
    prev = lambda s: jnp.maximum(s - 1, 0)
    row = lambda w: pl.BlockSpec((tm, w), lambda s: (cur(s), 0))
    row_shape = lambda w, dt: jax.ShapeDtypeStruct((b * t, w), dt)
    kv_spec = pl.BlockSpec((None, d, tm), lambda s: (cur(s) // n_tiles, 0, cur(s) % n_tiles))
    kv_shape = jax.ShapeDtypeStruct((b, d, t), F32)
    per_batch = lambda r: pl.BlockSpec((None, r, d), lambda s: (prev(s) // n_tiles, 0, 0))
    q, k, v, kb, vb, gl, y_rg, conv_new, hlast = pl.pallas_call(
        functools.partial(_in_proj_rglru_kernel, d=d, tm=tm, n_tiles=n_tiles),
        grid=(n + 1,),
        in_specs=[row(d), _resident(wts["w_in"].shape), per_batch(CONV_WIDTH - 1), per_batch(1)]
                 + [_resident(c.shape) for c in consts],
        out_specs=[row(d), kv_spec, kv_spec, row(d), row(d), row(2 * d),
                   pl.BlockSpec((tm, d), lambda s: (prev(s), 0)), per_batch(CONV_WIDTH - 1), per_batch(1)],
        out_shape=[row_shape(d, BF16), kv_shape, kv_shape, row_shape(d, BF16), row_shape(d, BF16),
                   row_shape(2 * d, F32), row_shape(d, BF16),
                   jax.ShapeDtypeStruct((b, CONV_WIDTH - 1, d), F32), jax.ShapeDtypeStruct((b, 1, d), F32)],
        scratch_shapes=[pltpu.VMEM((tm, d), F32)] * 2 + [pltpu.VMEM((tm, d), BF16)] + _rglru_scratch(tm, d),
        compiler_params=_params(("arbitrary",)),
        name="in_proj_rglru",
    )(x.reshape(b * t, d), wts["w_in"], conv0, h0.reshape(b, 1, d), *consts)
    k, v = (a.reshape(b, N_HEADS, HEAD_DIM, t).transpose(0, 3, 1, 2) for a in (k, v))
    return q, k, v, kb, vb, gl, y_rg, conv_new, hlast.reshape(b, d)


def _attn_subtile(q_ref, k_ref, v_ref, tri, acc_ref, carry_ref, z_ref, hl_ref, *, j, w, qs_blk, tq, tk, first):
    rows = pl.ds(j * tq if isinstance(j, int) else pl.multiple_of(j * tq, tq), tq)
    qs = qs_blk + j * tq
    hi_key = jnp.maximum(qs + tq - w * tk, 0)
    s = jnp.maximum(qs + tq - (w + 1) * tk, 0)
    win = pl.ds(s // KV_CHUNK, tk // KV_CHUNK)
    row_iota = lax.broadcasted_iota(jnp.int32, (tq, 1), 0)
    col_iota = lax.broadcasted_iota(jnp.int32, (tq, tk), 1)
    mask = col_iota < (jnp.minimum(qs + row_iota, hi_key) - s)
    mask2 = jnp.concatenate([mask, mask], axis=0)
    first_head = lax.broadcasted_iota(jnp.int32, (tq, LANES), 1) < HEAD_DIM
    t2 = HEADS_PER_SLAB * tq
    grp_rows = SLABS_PER_GROUP * t2

    later = []
    for grp in range(SLABS // SLABS_PER_GROUP):
        for sl in range(SLABS_PER_GROUP):
            slab = grp * SLABS_PER_GROUP + sl
            lanes = slice(slab * LANES, (slab + 1) * LANES)
            q2 = q_ref[rows, lanes]
            zero = jnp.zeros_like(q2)
            qq = jnp.concatenate([jnp.where(first_head, q2, zero), jnp.where(first_head, zero, q2)], axis=0)
            k2 = k_ref[win, :, lanes].reshape(tk, LANES)
            z = lax.dot_general(qq, k2, (((1,), (1,)), ((), ())), preferred_element_type=F32)
            neg_abs_z = pltpu.bitcast(pltpu.bitcast(z, jnp.uint32) | jnp.uint32(0x80000000), F32)
            sp = jnp.where(mask2, jnp.maximum(z, 0.0) + jnp.log(1.0 + jnp.exp(neg_abs_z)), 0.0)
            sp_hi = pltpu.bitcast(pltpu.bitcast(sp, jnp.uint32) & jnp.uint32(0xFFFF0000), F32)
            srows = slice(slab * t2, (slab + 1) * t2)
            z_ref[srows, :] = z
            hl_ref[srows, 0:tk] = sp_hi
            hl_ref[srows, tk:2 * tk] = sp - sp_hi
        later.append(lax.dot_general(hl_ref[grp * grp_rows:(grp + 1) * grp_rows, :], tri,
                                     (((1,), (0,)), ((), ())), preferred_element_type=F32))

    cmin = jnp.full((tq, 1), jnp.inf, F32)
    for slab in range(SLABS):
        lanes = slice(slab * LANES, (slab + 1) * LANES)
        sl = slab % SLABS_PER_GROUP
        lat = later[slab // SLABS_PER_GROUP][sl * t2:(sl + 1) * t2, :]
        x = z_ref[slab * t2:(slab + 1) * t2, :] - lat
        tot = lat[:, 0:1]
        ha, hb = slab * HEADS_PER_SLAB, slab * HEADS_PER_SLAB + 1
        if not first:
            c_old = jnp.concatenate([carry_ref[rows, ha:ha + 1], carry_ref[rows, hb:hb + 1]], axis=0)
            x = x - c_old
            tot = tot + c_old
        wgt = jnp.where(mask2, jnp.exp(x), 0.0).astype(BF16)
        o2 = jnp.dot(wgt, v_ref[win, :, lanes].reshape(tk, LANES), preferred_element_type=F32)
        o = jnp.where(first_head, o2[:tq], o2[tq:])
        if first:
            acc_ref[rows, lanes] = o
        else:
            acc_ref[rows, lanes] += o
        carry_ref[rows, ha:ha + 1] = tot[:tq]
        carry_ref[rows, hb:hb + 1] = tot[tq:]
        cmin = jnp.minimum(cmin, jnp.minimum(tot[:tq], tot[tq:]))
    return cmin


def _attn_kernel(q_ref, k_ref, v_ref, tri_ref, o_ref, acc_ref, carry_ref, z_ref, hl_ref, *, q_start, tqb, tq, tk):
    qs_blk = q_start + pl.program_id(1) * tqb
    tri = tri_ref[...]
    def sub(j, w, first):
        slot = j % 2 if isinstance(j, int) else 0
        return _attn_subtile(q_ref, k_ref, v_ref, tri, acc_ref, carry_ref, z_ref.at[slot], hl_ref.at[slot],
                             j=j, w=w, qs_blk=qs_blk, tq=tq, tk=tk, first=first)

    inf = jnp.full((tq, 1), jnp.inf, F32)

    def more_keys(w, cmin):
        return jnp.logical_and(jnp.min(cmin) < F32_EXP_UNDERFLOW, qs_blk + tqb - (w + 1) * tk > 0)

    cmin = inf
    for j in range(tqb // tq):
        cmin = jnp.minimum(cmin, sub(j=j, w=0, first=True))

    def older_window(state):
        w, _ = state
        cmin = lax.fori_loop(0, tqb // tq, lambda j, c: jnp.minimum(c, sub(j=j, w=w, first=False)), inf)
        return w + 1, more_keys(w, cmin)

    lax.while_loop(lambda st: st[1], older_window, (jnp.int32(1), more_keys(0, cmin)))
    o_ref[...] = acc_ref[...].astype(o_ref.dtype)


def _tri(tk):
    j = lax.broadcasted_iota(jnp.int32, (tk, tk), 0)
    s = lax.broadcasted_iota(jnp.int32, (tk, tk), 1)
    m = (j >= s).astype(BF16)
    return jnp.concatenate([m, m], axis=0)


def _attention(q, k, v, q_start, tqb, tq, tk):
    b, t_q, d = q.shape
    t_k = k.shape[1]
    assert d == N_HEADS * HEAD_DIM and t_k == q_start + t_q and t_k >= tk
    assert t_q % tqb == 0 and tqb % tq == 0 and tq % 16 == 0 and tk % 16 == 0 and q_start % 16 == 0
    assert t_k % KV_CHUNK == 0 and tk % KV_CHUNK == 0
    k, v = (a.reshape(b, t_k // KV_CHUNK, KV_CHUNK, d) for a in (k, v))
    kv_spec = pl.BlockSpec((None, t_k // KV_CHUNK, KV_CHUNK, d), lambda bi, i: (bi, 0, 0, 0),
                           pipeline_mode=pl.Buffered(1))
    return pl.pallas_call(
        functools.partial(_attn_kernel, q_start=q_start, tqb=tqb, tq=tq, tk=tk),
        grid=(b, t_q // tqb),
        in_specs=[pl.BlockSpec((None, tqb, d), lambda bi, i: (bi, i, 0)), kv_spec, kv_spec,
                  _resident((2 * tk, tk))],
        out_specs=pl.BlockSpec((None, tqb, d), lambda bi, i: (bi, i, 0)),
        out_shape=jax.ShapeDtypeStruct((b, t_q, d), BF16),
        scratch_shapes=[pltpu.VMEM((tqb, d), F32), pltpu.VMEM((tqb, N_HEADS), F32),
                        pltpu.VMEM((2, N_HEADS * tq, tk), F32), pltpu.VMEM((2, N_HEADS * tq, 2 * tk), F32)],
        compiler_params=_params(("parallel", "arbitrary")),
        name="stickbreak_attn",
    )(q, k, v, _tri(tk))


def _layer_norm(x, g, b):
    mu = jnp.mean(x, axis=-1, keepdims=True)
    xc = x - mu
    var = jnp.mean(xc * xc, axis=-1, keepdims=True)
    return xc * lax.rsqrt(var + LN_EPS) * g + b


def _merge_ffn_kernel(x_ref, gl_ref, ysb_ref, yrg_ref, bm_ref, wsb_ref, wrg_ref, wo_ref, ln1_ref,
                      wgu_ref, wd_ref, ln2_ref, y_ref, f_ref, *, d, dff, alpha):
    ya = jnp.dot(ysb_ref[...], wsb_ref[...], preferred_element_type=F32)
    yb = jnp.dot(yrg_ref[...], wrg_ref[...], preferred_element_type=F32)
    g = jax.nn.sigmoid(gl_ref[...] + bm_ref[...])
    m = (g[:, :d] * ya + g[:, d:] * yb).astype(BF16)
    h1 = _layer_norm(alpha * x_ref[...] + jnp.dot(m, wo_ref[...], preferred_element_type=F32),
                     ln1_ref[0:1, :], ln1_ref[1:2, :])
    h1b = h1.astype(BF16)
    for c in range(dff // FF_CHUNK):
        cols = slice(c * FF_CHUNK, (c + 1) * FF_CHUNK)
        gate = jnp.dot(h1b, wgu_ref[0, :, cols], preferred_element_type=F32)
        up = jnp.dot(h1b, wgu_ref[1, :, cols], preferred_element_type=F32)
        f_ref[:, cols] = (jax.nn.silu(gate) * up).astype(BF16)
    f = jnp.dot(f_ref[...], wd_ref[...], preferred_element_type=F32)
    y_ref[...] = _layer_norm(alpha * h1 + f, ln2_ref[0:1, :], ln2_ref[1:2, :])


def _merge_ffn(x2d, gl, y_sb, y_rg, wts, tm):
    n, d = x2d.shape
    dff = wts["w_down"].shape[0]
    assert n % tm == 0 and dff % FF_CHUNK == 0
    consts = [wts["b_merge"].reshape(1, -1), wts["w_sb_proj"], wts["w_rg_proj"], wts["w_o"], wts["ln1"],
              wts["w_gu"], wts["w_down"], wts["ln2"]]
    row = lambda w: pl.BlockSpec((tm, w), lambda i: (i, 0))
    return pl.pallas_call(
        functools.partial(_merge_ffn_kernel, d=d, dff=dff, alpha=wts["alpha"]),
        grid=(n // tm,),
        in_specs=[row(d), row(N_BRANCHES * d), row(d), row(d)] + [_resident(c.shape) for c in consts],
        out_specs=row(d),
        out_shape=jax.ShapeDtypeStruct((n, d), F32),
        scratch_shapes=[pltpu.VMEM((tm, dff), BF16)],
        compiler_params=_params(("parallel",)),
        name="merge_ffn",
    )(x2d, gl, y_sb, y_rg, *consts)


def _layer(x, past_k, past_v, conv0, h0, wts, *, tm_proj, tqb, tq, tk, tm_ffn):
    b, t, d = x.shape
    n = b * t
    if past_k is None:
        q, k, v, kb, vb, gl, y_rg, conv_new, h_last = _in_proj_rglru(x, conv0, h0, wts, tm_proj)
        q_start = 0
    else:
        q, k, v, kb, vb, xr, xg, gl = _in_proj(x.reshape(n, d), wts["w_in"], tm_proj)
        k, v = (a.reshape(b, t, N_HEADS, HEAD_DIM) for a in (k, v))
        xr = xr.reshape(b, t, d)
        y_rg, h_last = _rglru(xr, xg.reshape(b, t, d), conv0, h0, wts, t)
        conv_new = xr[:, t - (CONV_WIDTH - 1):, :]
        q_start = past_k.shape[1]
        kb = jnp.concatenate([past_k.reshape(b, q_start, d).astype(BF16), kb.reshape(b, t, d)], axis=1)
        vb = jnp.concatenate([past_v.reshape(b, q_start, d).astype(BF16), vb.reshape(b, t, d)], axis=1)
    seq = lambda a: a.reshape(b, -1, d)
    y_sb = _attention(seq(q), seq(kb), seq(vb), q_start, tqb, tq, min(tk, q_start + t))
    y = _merge_ffn(x.reshape(n, d), gl.reshape(n, -1), y_sb.reshape(n, d), y_rg.reshape(n, d), wts, tm_ffn)
    return y.reshape(b, t, d), k, v, conv_new, h_last


def kernel(x_prompt, x_sample, cache_k, cache_v, state_conv, state_h, w_in, b_merge, w_sb_proj, w_rg_proj, conv_w, conv_b, w_rg_a, b_rg_a, w_rg_x, b_rg_x, lam, w_o, ln1_g, ln1_b, w_gate, w_up, w_down, ln2_g, ln2_b):
    depth = 1
    wts = dict(
        w_in=w_in.astype(BF16), b_merge=b_merge, w_sb_proj=w_sb_proj.astype(BF16),
        w_rg_proj=w_rg_proj.astype(BF16), conv_w=conv_w, conv_b=conv_b, w_rg_a=w_rg_a, b_rg_a=b_rg_a,
        w_rg_x=w_rg_x, b_rg_x=b_rg_x, lam=lam, w_o=w_o.astype(BF16), ln1=jnp.stack([ln1_g, ln1_b]),
        w_gu=jnp.stack([w_gate, w_up]).astype(BF16), w_down=w_down.astype(BF16),
        ln2=jnp.stack([ln2_g, ln2_b]), alpha=(2.0 * depth) ** 0.25)

    bp, tp, d = x_prompt.shape
    zero_conv = jnp.zeros((bp, CONV_WIDTH - 1, d), x_prompt.dtype)
    zero_h = jnp.zeros((bp, d), x_prompt.dtype)
    y_p, k_p, v_p, conv_p, h_p = _layer(
        x_prompt, None, None, zero_conv, zero_h, wts,
        tm_proj=min(256, tp), tqb=min(256, tp), tq=min(64, tp), tk=256, tm_ffn=min(512, bp * tp))

    bs, ts, _ = x_sample.shape
    y_s, k_s, v_s, conv_s, h_s = _layer(
        x_sample, cache_k, cache_v, state_conv, state_h, wts,
        tm_proj=bs * ts, tqb=ts, tq=ts, tk=256, tm_ffn=bs * ts)

    return (y_p, y_s, k_p, v_p, conv_p, h_p, k_s, v_s, conv_s, h_s)
```

```python
import functools

import jax
import jax.numpy as jnp
from jax import lax
from jax.experimental import pallas as pl
from jax.experimental.pallas import tpu as pltpu

F32 = jnp.float32
BF16 = jnp.bfloat16

N_HEADS = 16
HEAD_DIM = 64
LANES = 128
SUBLANES = 8
HEADS_PER_SLAB = LANES // HEAD_DIM
SLABS = N_HEADS // HEADS_PER_SLAB
SLABS_PER_GROUP = 4
KV_CHUNK = 16
CONV_WIDTH = 4
RG_C = 8.0
LN_EPS = 1e-5
N_BRANCHES = 2
GATE_CHUNK = 256
SCAN_ROWS = 64
FF_CHUNK = 256
F32_EXP_UNDERFLOW = 104.0
VMEM_LIMIT = 56 * 1024 * 1024


def _params(sem):
    return pltpu.CompilerParams(dimension_semantics=sem, vmem_limit_bytes=VMEM_LIMIT)


def _resident(shape):
    nd = len(shape)
    return pl.BlockSpec(shape, lambda *_: (0,) * nd, pipeline_mode=pl.Buffered(1))


def _rglru_stages(xr_ref, xg_ref, cw_ref, cb_ref, wg_ref, bg_ref, lam_ref, hist_ref, xc_ref, g_ref, h_ref,
                  store_y, *, tt, d):
    hist = CONV_WIDTH - 1
    nchunk = d // GATE_CHUNK

    def conv_and_gates():
        x = xr_ref[...]
        first_row = lax.broadcasted_iota(jnp.int32, (SUBLANES, d), 0) == 0
        u = None
        for j in range(hist):
            u_prev = cw_ref[j:j + 1, :] * hist_ref[hist - 1:hist, :]
            for m in range(j):
                u_prev = u_prev + cw_ref[m:m + 1, :] * hist_ref[hist - 1 - (j - m):hist - (j - m), :]
            u = cw_ref[j:j + 1, :] * x if u is None else cw_ref[j:j + 1, :] * x + u
            u = pltpu.roll(u, 1, 0)
            u = jnp.concatenate([jnp.where(first_row, u_prev, u[:SUBLANES]), u[SUBLANES:]], axis=0)
        xc = cb_ref[...] + (cw_ref[hist:hist + 1, :] * x + u)
        hist_ref[...] = xr_ref[tt - hist:tt, :]
        xc_ref[...] = xc
        xcb = xc.astype(BF16)
        for c in range(nchunk):
            g_ref[:, 2 * c * GATE_CHUNK:2 * (c + 1) * GATE_CHUNK] = jnp.dot(
                xcb[:, c * GATE_CHUNK:(c + 1) * GATE_CHUNK], wg_ref[c], preferred_element_type=F32)

    n = min(SCAN_ROWS, tt)
    assert tt % n == 0 and n % SUBLANES == 0

    def recur(k):
        rows = slice(k * n, (k + 1) * n)
        sub = lax.broadcasted_iota(jnp.int32, (SUBLANES, GATE_CHUNK), 0)
        for c in range(nchunk):
            cols = slice(c * GATE_CHUNK, (c + 1) * GATE_CHUNK)
            pre = lambda half: g_ref[rows, (2 * c + half) * GATE_CHUNK:(2 * c + half + 1) * GATE_CHUNK]
            r = jax.nn.sigmoid(pre(0) + bg_ref[0:1, cols])
            ig = jax.nn.sigmoid(pre(1) + bg_ref[1:2, cols])
            log_a = r * ((-RG_C) * jax.nn.softplus(-lam_ref[:, cols]))
            a = jnp.exp(log_a)
            one_minus_a2 = jnp.tanh(-log_a) * (a * a + 1.0)
            root = jnp.where(one_minus_a2 > 0.0, one_minus_a2 * lax.rsqrt(one_minus_a2), 0.0)
            b = root * (ig * xc_ref[rows, cols])
            hprev = h_ref[:, cols]
            hs = []
            for g8 in range(n // SUBLANES):
                grp = slice(g8 * SUBLANES, (g8 + 1) * SUBLANES)
                ag, bgp = a[grp], b[grp]
                for sh in (1, 2, 4):
                    keep = sub >= sh
                    a_sh = jnp.where(keep, pltpu.roll(ag, sh, 0), 1.0)
                    b_sh = jnp.where(keep, pltpu.roll(bgp, sh, 0), 0.0)
                    bgp = ag * b_sh + bgp
                    ag = ag * a_sh
                h = ag * hprev + bgp
                hs.append(h)
                hprev = h[SUBLANES - 1:SUBLANES, :]
            h_ref[:, cols] = hprev
            store_y(rows, cols, (jnp.concatenate(hs, axis=0) * jax.nn.gelu(xg_ref[rows, cols])).astype(BF16))

    return [conv_and_gates] + [functools.partial(recur, k) for k in range(tt // n)]


def _rglru_kernel(xr_ref, xg_ref, conv0_ref, h0_ref, cw_ref, cb_ref, wg_ref, bg_ref, lam_ref,
                  y_ref, hlast_ref, hist_ref, xc_ref, g_ref, h_ref, *, tt, d):
    @pl.when(pl.program_id(1) == 0)
    def _():
        hist_ref[...] = conv0_ref[...]
        h_ref[...] = h0_ref[...]

    def store_y(rows, cols, y):
        y_ref[rows, cols] = y

    for stage in _rglru_stages(xr_ref, xg_ref, cw_ref, cb_ref, wg_ref, bg_ref, lam_ref, hist_ref, xc_ref,
                               g_ref, h_ref, store_y, tt=tt, d=d):
        stage()
    hlast_ref[...] = h_ref[...]


def _pack_gate_weights(w_a, w_x):
    nb, blk, _ = w_a.shape
    per = GATE_CHUNK // blk

    def bd(w):
        w = w.reshape(nb // per, per, blk, blk)
        eye = jnp.eye(per, dtype=w.dtype)
        return jnp.einsum("cpij,pq->cpiqj", w, eye).reshape(nb // per, GATE_CHUNK, GATE_CHUNK)

    return jnp.concatenate([bd(w_a), bd(w_x)], axis=2).astype(BF16)


def _rglru_operands(wts, d):
    wg = _pack_gate_weights(wts["w_rg_a"], wts["w_rg_x"])
    bg = jnp.stack([wts["b_rg_a"].reshape(d), wts["b_rg_x"].reshape(d)])
    return [wts["conv_w"], wts["conv_b"].reshape(1, d), wg, bg, wts["lam"].reshape(1, d)]


def _rglru_scratch(tt, d):
    return [pltpu.VMEM((CONV_WIDTH - 1, d), F32), pltpu.VMEM((tt, d), F32),
            pltpu.VMEM((tt, N_BRANCHES * d), F32), pltpu.VMEM((1, d), F32)]


def _rglru(xr, xg, conv0, h0, wts, tt):
    b, t, d = xr.shape
    assert t % tt == 0 and tt % SUBLANES == 0 and t >= CONV_WIDTH - 1
    consts = _rglru_operands(wts, d)
    seq = pl.BlockSpec((None, tt, d), lambda bi, i: (bi, i, 0))
    per_batch = lambda r: pl.BlockSpec((None, r, d), lambda bi, i: (bi, 0, 0))
    y, hlast = pl.pallas_call(
        functools.partial(_rglru_kernel, tt=tt, d=d),
        grid=(b, t // tt),
        in_specs=[seq, seq, per_batch(CONV_WIDTH - 1), per_batch(1)] + [_resident(c.shape) for c in consts],
        out_specs=[seq, per_batch(1)],
        out_shape=[jax.ShapeDtypeStruct((b, t, d), BF16), jax.ShapeDtypeStruct((b, 1, d), F32)],
        scratch_shapes=_rglru_scratch(tt, d),
        compiler_params=_params(("parallel", "arbitrary")),
        name="rglru",
    )(xr, xg, conv0, h0.reshape(b, 1, d), *consts)
    return y, hlast.reshape(b, d)


PROJ_CHUNK = 256


def _project_stages(x_ref, w_ref, q_ref, k_ref, v_ref, kb_ref, vb_ref, xr_ref, xg_ref, gl_ref, xb_ref, *,
                    d, tm, time_minor):
    def load():
        xb_ref[...] = x_ref[...].astype(BF16)

    def mm(c0):
        return jnp.dot(xb_ref[...], w_ref[:, c0:c0 + PROJ_CHUNK], preferred_element_type=F32)

    def q_chunk(c0):
        q_ref[:, c0:c0 + PROJ_CHUNK] = (mm(c0) * (HEAD_DIM ** -0.5)).astype(BF16)

    def kv_chunk(o_ref, ob_ref, base, c0):
        u = mm(base + c0)
        ob_ref[:, c0:c0 + PROJ_CHUNK] = u.astype(BF16)
        if time_minor:
            o_ref[c0:c0 + PROJ_CHUNK, :] = u.T
        else:
            for j in range(PROJ_CHUNK // HEAD_DIM):
                h = c0 // HEAD_DIM + j
                o_ref[pl.ds(h, tm, stride=N_HEADS), :] = u[:, j * HEAD_DIM:(j + 1) * HEAD_DIM]

    def plain_chunk(o_ref, base, c0):
        o_ref[:, c0:c0 + PROJ_CHUNK] = mm(base + c0)

    cols = range(0, d, PROJ_CHUNK)
    cast = ([functools.partial(q_chunk, c0) for c0 in cols]
            + [functools.partial(kv_chunk, k_ref, kb_ref, d, c0) for c0 in cols]
            + [functools.partial(kv_chunk, v_ref, vb_ref, 2 * d, c0) for c0 in cols])
    gl = [functools.partial(plain_chunk, gl_ref, 5 * d, c0) for c0 in range(0, 2 * d, PROJ_CHUNK)]
    xr = [functools.partial(plain_chunk, xr_ref, 3 * d, c0) for c0 in cols]
    xg = [functools.partial(plain_chunk, xg_ref, 4 * d, c0) for c0 in cols]
    return load, cast + gl, xr, xg


def _in_proj_kernel(*refs, d, tm):
    load, qkv_gl, xr, xg = _project_stages(*refs, d=d, tm=tm, time_minor=False)
    load()
    for stage in qkv_gl + xr + xg:
        stage()


def _in_proj(x2d, w_in_bf, tm):
    n, d = x2d.shape
    assert n % tm == 0
    row = lambda w: pl.BlockSpec((tm, w), lambda i: (i, 0))
    row_shape = lambda w, dt: jax.ShapeDtypeStruct((n, w), dt)
    kv_spec = pl.BlockSpec((tm * N_HEADS, HEAD_DIM), lambda i: (i, 0))
    kv_shape = jax.ShapeDtypeStruct((n * N_HEADS, HEAD_DIM), F32)
    q, k, v, kb, vb, xr, xg, gl = pl.pallas_call(
        functools.partial(_in_proj_kernel, d=d, tm=tm),
        grid=(n // tm,),
        in_specs=[row(d), _resident(w_in_bf.shape)],
        out_specs=[row(d), kv_spec, kv_spec, row(d), row(d), row(d), row(d), row(2 * d)],
        out_shape=[row_shape(d, BF16), kv_shape, kv_shape, row_shape(d, BF16), row_shape(d, BF16),
                   row_shape(d, F32), row_shape(d, F32), row_shape(2 * d, F32)],
        scratch_shapes=[pltpu.VMEM((tm, d), BF16)],
        compiler_params=_params(("parallel",)),
        name="in_proj",
    )(x2d, w_in_bf)
    k, v = (a.reshape(n, N_HEADS, HEAD_DIM) for a in (k, v))
    return q, k, v, kb, vb, xr, xg, gl


def _in_proj_rglru_kernel(x_ref, w_ref, conv0_ref, h0_ref, cw_ref, cb_ref, wg_ref, bg_ref, lam_ref,
                          q_ref, k_ref, v_ref, kb_ref, vb_ref, gl_ref, yrg_ref, conv_ref, hlast_ref,
                          xr_ref, xg_ref, xb_ref, hist_ref, xc_ref, g_ref, h_ref, *, d, tm, n_tiles):
    s = pl.program_id(0)
    prev = jnp.maximum(s - 1, 0)

    @pl.when(s == 0)
    def _():
        xr_ref[...] = jnp.zeros_like(xr_ref)
        xg_ref[...] = jnp.zeros_like(xg_ref)

    @pl.when(prev % n_tiles == 0)
    def _():
        hist_ref[...] = conv0_ref[...]
        h_ref[...] = h0_ref[...]

    def store_y(rows, cols, y):
        yrg_ref[rows, cols] = y

    load, qkv_gl, xr, xg = _project_stages(x_ref, w_ref, q_ref, k_ref, v_ref, kb_ref, vb_ref, xr_ref, xg_ref,
                                           gl_ref, xb_ref, d=d, tm=tm, time_minor=True)
    rg = _rglru_stages(xr_ref, xg_ref, cw_ref, cb_ref, wg_ref, bg_ref, lam_ref, hist_ref, xc_ref,
                       g_ref, h_ref, store_y, tt=tm, d=d)
    load()
    for stage in qkv_gl + rg + xr + xg:
        stage()

    @pl.when(prev % n_tiles == n_tiles - 1)
    def _():
        conv_ref[...] = hist_ref[...]
        hlast_ref[...] = h_ref[...]


def _in_proj_rglru(x, conv0, h0, wts, tm):
    b, t, d = x.shape
    n_tiles = t // tm
    n = b * n_tiles
    assert t % tm == 0 and tm % LANES == 0 and t >= CONV_WIDTH - 1
    consts = _rglru_operands(wts, d)
    cur = lambda s: jnp.minimum(s, n - 1)
    prev = lambda s: jnp.maximum(s - 1, 0)
    row = lambda w: pl.BlockSpec((tm, w), lambda s: (cur(s), 0))
    row_shape = lambda w, dt: jax.ShapeDtypeStruct((b * t, w), dt)
    kv_spec = pl.BlockSpec((None, d, tm), lambda s: (cur(s) // n_tiles, 0, cur(s) % n_tiles))
    kv_shape = jax.ShapeDtypeStruct((b, d, t), F32)
    per_batch = lambda r: pl.BlockSpec((None, r, d), lambda s: (prev(s) // n_tiles, 0, 0))
    q, k, v, kb, vb, gl, y_rg, conv_new, hlast = pl.pallas_call(
        functools.partial(_in_proj_rglru_kernel, d=d, tm=tm, n_tiles=n_tiles),
        grid=(n + 1,),
        in_specs=[row(d), _resident(wts["w_in"].shape), per_batch(CONV_WIDTH - 1), per_batch(1)]
                 + [_resident(c.shape) for c in consts],
        out_specs=[row(d), kv_spec, kv_spec, row(d), row(d), row(2 * d),
                   pl.BlockSpec((tm, d), lambda s: (prev(s), 0)), per_batch(CONV_WIDTH - 1), per_batch(1)],
        out_shape=[row_shape(d, BF16), kv_shape, kv_shape, row_shape(d, BF16), row_shape(d, BF16),
                   row_shape(2 * d, F32), row_shape(d, BF16),
                   jax.ShapeDtypeStruct((b, CONV_WIDTH - 1, d), F32), jax.ShapeDtypeStruct((b, 1, d), F32)],
        scratch_shapes=[pltpu.VMEM((tm, d), F32)] * 2 + [pltpu.VMEM((tm, d), BF16)] + _rglru_scratch(tm, d),
        compiler_params=_params(("arbitrary",)),
        name="in_proj_rglru",
    )(x.reshape(b * t, d), wts["w_in"], conv0, h0.reshape(b, 1, d), *consts)
    k, v = (a.reshape(b, N_HEADS, HEAD_DIM, t).transpose(0, 3, 1, 2) for a in (k, v))
    return q, k, v, kb, vb, gl, y_rg, conv_new, hlast.reshape(b, d)


def _attn_subtile(q_ref, k_ref, v_ref, tri, acc_ref, carry_ref, z_ref, hl_ref, *, j, w, qs_blk, tq, tk, first):
    rows = pl.ds(j * tq if isinstance(j, int) else pl.multiple_of(j * tq, tq), tq)
    qs = qs_blk + j * tq
    hi_key = jnp.maximum(qs + tq - w * tk, 0)
    s = jnp.maximum(qs + tq - (w + 1) * tk, 0)
    win = pl.ds(s // KV_CHUNK, tk // KV_CHUNK)
    row_iota = lax.broadcasted_iota(jnp.int32, (tq, 1), 0)
    col_iota = lax.broadcasted_iota(jnp.int32, (tq, tk), 1)
    mask = col_iota < (jnp.minimum(qs + row_iota, hi_key) - s)
    mask2 = jnp.concatenate([mask, mask], axis=0)
    first_head = lax.broadcasted_iota(jnp.int32, (tq, LANES), 1) < HEAD_DIM
    t2 = HEADS_PER_SLAB * tq
    grp_rows = SLABS_PER_GROUP * t2

    later = []
    for grp in range(SLABS // SLABS_PER_GROUP):
        for sl in range(SLABS_PER_GROUP):
            slab = grp * SLABS_PER_GROUP + sl
            lanes = slice(slab * LANES, (slab + 1) * LANES)
            q2 = q_ref[rows, lanes]
            zero = jnp.zeros_like(q2)
            qq = jnp.concatenate([jnp.where(first_head, q2, zero), jnp.where(first_head, zero, q2)], axis=0)
            k2 = k_ref[win, :, lanes].reshape(tk, LANES)
            z = lax.dot_general(qq, k2, (((1,), (1,)), ((), ())), preferred_element_type=F32)
            neg_abs_z = pltpu.bitcast(pltpu.bitcast(z, jnp.uint32) | jnp.uint32(0x80000000), F32)
            sp = jnp.where(mask2, jnp.maximum(z, 0.0) + jnp.log(1.0 + jnp.exp(neg_abs_z)), 0.0)
            sp_hi = pltpu.bitcast(pltpu.bitcast(sp, jnp.uint32) & jnp.uint32(0xFFFF0000), F32)
            srows = slice(slab * t2, (slab + 1) * t2)
            z_ref[srows, :] = z
            hl_ref[srows, 0:tk] = sp_hi
            hl_ref[srows, tk:2 * tk] = sp - sp_hi
        later.append(lax.dot_general(hl_ref[grp * grp_rows:(grp + 1) * grp_rows, :], tri,
                                     (((1,), (0,)), ((), ())), preferred_element_type=F32))

    cmin = jnp.full((tq, 1), jnp.inf, F32)
    for slab in range(SLABS):
        lanes = slice(slab * LANES, (slab + 1) * LANES)
        sl = slab % SLABS_PER_GROUP
        lat = later[slab // SLABS_PER_GROUP][sl * t2:(sl + 1) * t2, :]
        x = z_ref[slab * t2:(slab + 1) * t2, :] - lat
        tot = lat[:, 0:1]
        ha, hb = slab * HEADS_PER_SLAB, slab * HEADS_PER_SLAB + 1
        if not first:
            c_old = jnp.concatenate([carry_ref[rows, ha:ha + 1], carry_ref[rows, hb:hb + 1]], axis=0)
            x = x - c_old
            tot = tot + c_old
        wgt = jnp.where(mask2, jnp.exp(x), 0.0).astype(BF16)
        o2 = jnp.dot(wgt, v_ref[win, :, lanes].reshape(tk, LANES), preferred_element_type=F32)
        o = jnp.where(first_head, o2[:tq], o2[tq:])
        if first:
            acc_ref[rows, lanes] = o
        else:
            acc_ref[rows, lanes] += o
        carry_ref[rows, ha:ha + 1] = tot[:tq]
        carry_ref[rows, hb:hb + 1] = tot[tq:]
        cmin = jnp.minimum(cmin, jnp.minimum(tot[:tq], tot[tq:]))
    return cmin


def _attn_kernel(q_ref, k_ref, v_ref, tri_ref, o_ref, acc_ref, carry_ref, z_ref, hl_ref, *, q_start, tqb, tq, tk):
    qs_blk = q_start + pl.program_id(1) * tqb
    tri = tri_ref[...]
    def sub(j, w, first):
        slot = j % 2 if isinstance(j, int) else 0
        return _attn_subtile(q_ref, k_ref, v_ref, tri, acc_ref, carry_ref, z_ref.at[slot], hl_ref.at[slot],
                             j=j, w=w, qs_blk=qs_blk, tq=tq, tk=tk, first=first)

    inf = jnp.full((tq, 1), jnp.inf, F32)

    def more_keys(w, cmin):
        return jnp.logical_and(jnp.min(cmin) < F32_EXP_UNDERFLOW, qs_blk + tqb - (w + 1) * tk > 0)

    cmin = inf
    for j in range(tqb // tq):
        cmin = jnp.minimum(cmin, sub(j=j, w=0, first=True))

    def older_window(state):
        w, _ = state
        cmin = lax.fori_loop(0, tqb // tq, lambda j, c: jnp.minimum(c, sub(j=j, w=w, first=False)), inf)
        return w + 1, more_keys(w, cmin)

    lax.while_loop(lambda st: st[1], older_window, (jnp.int32(1), more_keys(0, cmin)))
    o_ref[...] = acc_ref[...].astype(o_ref.dtype)


def _tri(tk):
    j = lax.broadcasted_iota(jnp.int32, (tk, tk), 0)
    s = lax.broadcasted_iota(jnp.int32, (tk, tk), 1)
    m = (j >= s).astype(BF16)
    return jnp.concatenate([m, m], axis=0)


def _attention(q, k, v, q_start, tqb, tq, tk):
    b, t_q, d = q.shape
    t_k = k.shape[1]
    assert d == N_HEADS * HEAD_DIM and t_k == q_start + t_q and t_k >= tk
    assert t_q % tqb == 0 and tqb % tq == 0 and tq % 16 == 0 and tk % 16 == 0 and q_start % 16 == 0
    assert t_k % KV_CHUNK == 0 and tk % KV_CHUNK == 0
    k, v = (a.reshape(b, t_k // KV_CHUNK, KV_CHUNK, d) for a in (k, v))
    kv_spec = pl.BlockSpec((None, t_k // KV_CHUNK, KV_CHUNK, d), lambda bi, i: (bi, 0, 0, 0),
                           pipeline_mode=pl.Buffered(1))
    return pl.pallas_call(
        functools.partial(_attn_kernel, q_start=q_start, tqb=tqb, tq=tq, tk=tk),
        grid=(b, t_q // tqb),
        in_specs=[pl.BlockSpec((None, tqb, d), lambda bi, i: (bi, i, 0)), kv_spec, kv_spec,
                  _resident((2 * tk, tk))],
        out_specs=pl.BlockSpec((None, tqb, d), lambda bi, i: (bi, i, 0)),
        out_shape=jax.ShapeDtypeStruct((b, t_q, d), BF16),
        scratch_shapes=[pltpu.VMEM((tqb, d), F32), pltpu.VMEM((tqb, N_HEADS), F32),
                        pltpu.VMEM((2, N_HEADS * tq, tk), F32), pltpu.VMEM((2, N_HEADS * tq, 2 * tk), F32)],
        compiler_params=_params(("parallel", "arbitrary")),
        name="stickbreak_attn",
    )(q, k, v, _tri(tk))


def _layer_norm(x, g, b):
    mu = jnp.mean(x, axis=-1, keepdims=True)
    xc = x - mu
    var = jnp.mean(xc * xc, axis=-1, keepdims=True)
    return xc * lax.rsqrt(var + LN_EPS) * g + b


def _merge_ffn_kernel(x_ref, gl_ref, ysb_ref, yrg_ref, bm_ref, wsb_ref, wrg_ref, wo_ref, ln1_ref,
                      wgu_ref, wd_ref, ln2_ref, y_ref, f_ref, *, d, dff, alpha):
    ya = jnp.dot(ysb_ref[...], wsb_ref[...], preferred_element_type=F32)
    yb = jnp.dot(yrg_ref[...], wrg_ref[...], preferred_element_type=F32)
    g = jax.nn.sigmoid(gl_ref[...] + bm_ref[...])
    m = (g[:, :d] * ya + g[:, d:] * yb).astype(BF16)
    h1 = _layer_norm(alpha * x_ref[...] + jnp.dot(m, wo_ref[...], preferred_element_type=F32),
                     ln1_ref[0:1, :], ln1_ref[1:2, :])
    h1b = h1.astype(BF16)
    for c in range(dff // FF_CHUNK):
        cols = slice(c * FF_CHUNK, (c + 1) * FF_CHUNK)
        gate = jnp.dot(h1b, wgu_ref[0, :, cols], preferred_element_type=F32)
        up = jnp.dot(h1b, wgu_ref[1, :, cols], preferred_element_type=F32)
        f_ref[:, cols] = (jax.nn.silu(gate) * up).astype(BF16)
    f = jnp.dot(f_ref[...], wd_ref[...], preferred_element_type=F32)
    y_ref[...] = _layer_norm(alpha * h1 + f, ln2_ref[0:1, :], ln2_ref[1:2, :])


def _merge_ffn(x2d, gl, y_sb, y_rg, wts, tm):
    n, d = x2d.shape
    dff = wts["w_down"].shape[0]
    assert n % tm == 0 and dff % FF_CHUNK == 0
    consts = [wts["b_merge"].reshape(1, -1), wts["w_sb_proj"], wts["w_rg_proj"], wts["w_o"], wts["ln1"],
              wts["w_gu"], wts["w_down"], wts["ln2"]]
    row = lambda w: pl.BlockSpec((tm, w), lambda i: (i, 0))
    return pl.pallas_call(
        functools.partial(_merge_ffn_kernel, d=d, dff=dff, alpha=wts["alpha"]),
        grid=(n // tm,),
        in_specs=[row(d), row(N_BRANCHES * d), row(d), row(d)] + [_resident(c.shape) for c in consts],
        out_specs=row(d),
        out_shape=jax.ShapeDtypeStruct((n, d), F32),
        scratch_shapes=[pltpu.VMEM((tm, dff), BF16)],
        compiler_params=_params(("parallel",)),
        name="merge_ffn",
    )(x2d, gl, y_sb, y_rg, *consts)


def _layer(x, past_k, past_v, conv0, h0, wts, *, tm_proj, tqb, tq, tk, tm_ffn):
    b, t, d = x.shape
    n = b * t
    if past_k is None:
        q, k, v, kb, vb, gl, y_rg, conv_new, h_last = _in_proj_rglru(x, conv0, h0, wts, tm_proj)
        q_start = 0
    else:
        q, k, v, kb, vb, xr, xg, gl = _in_proj(x.reshape(n, d), wts["w_in"], tm_proj)
        k, v = (a.reshape(b, t, N_HEADS, HEAD_DIM) for a in (k, v))
        xr = xr.reshape(b, t, d)
        y_rg, h_last = _rglru(xr, xg.reshape(b, t, d), conv0, h0, wts, t)
        conv_new = xr[:, t - (CONV_WIDTH - 1):, :]
        q_start = past_k.shape[1]
        kb = jnp.concatenate([past_k.reshape(b, q_start, d).astype(BF16), kb.reshape(b, t, d)], axis=1)
        vb = jnp.concatenate([past_v.reshape(b, q_start, d).astype(BF16), vb.reshape(b, t, d)], axis=1)
    seq = lambda a: a.reshape(b, -1, d)
    y_sb = _attention(seq(q), seq(kb), seq(vb), q_start, tqb, tq, min(tk, q_start + t))
    y = _merge_ffn(x.reshape(n, d), gl.reshape(n, -1), y_sb.reshape(n, d), y_rg.reshape(n, d), wts, tm_ffn)
    return y.reshape(b, t, d), k, v, conv_new, h_last


def kernel(x_prompt, x_sample, cache_k, cache_v, state_conv, state_h, w_in, b_merge, w_sb_proj, w_rg_proj, conv_w, conv_b, w_rg_a, b_rg_a, w_rg_x, b_rg_x, lam, w_o, ln1_g, ln1_b, w_gate, w_up, w_down, ln2_g, ln2_b):
    depth = 1
    wts = dict(
        w_in=w_in.astype(BF16), b_merge=b_merge, w_sb_proj=w_sb_proj.astype(BF16),
        w_rg_proj=w_rg_proj.astype(BF16), conv_w=conv_w, conv_b=conv_b, w_rg_a=w_rg_a, b_rg_a=b_rg_a,
        w_rg_x=w_rg_x, b_rg_x=b_rg_x, lam=lam, w_o=w_o.astype(BF16), ln1=jnp.stack([ln1_g, ln1_b]),
        w_gu=jnp.stack([w_gate, w_up]).astype(BF16), w_down=w_down.astype(BF16),
        ln2=jnp.stack([ln2_g, ln2_b]), alpha=(2.0 * depth) ** 0.25)

    bp, tp, d = x_prompt.shape
    zero_conv = jnp.zeros((bp, CONV_WIDTH - 1, d), x_prompt.dtype)
    zero_h = jnp.zeros((bp, d), x_prompt.dtype)
    y_p, k_p, v_p, conv_p, h_p = _layer(
        x_prompt, None, None, zero_conv, zero_h, wts,
        tm_proj=min(256, tp), tqb=min(256, tp), tq=min(64, tp), tk=256, tm_ffn=min(512, bp * tp))

    bs, ts, _ = x_sample.shape
    y_s, k_s, v_s, conv_s, h_s = _layer(
        x_sample, cache_k, cache_v, state_conv, state_h, wts,
        tm_proj=bs * ts, tqb=ts, tq=ts, tk=256, tm_ffn=bs * ts)

    return (y_p, y_s, k_p, v_p, conv_p, h_p, k_s, v_s, conv_s, h_s)
```
